```python
import jax, jax.numpy as jnp
from jax import lax
import numpy as np

D_MODEL = 2048
BATCH = 8
SEQ = 4096
DEPTH = 2

GRID_W = 64
CTX_LEN = 256
HEAD_DIM = 128
BLOCK = 128
EPS = 1e-6
NEG_INF = -1e30
ATTN_SCALE = HEAD_DIM ** -0.5
POOL_WINDOWS = (2, 4, 8, 16)
N_POOL_GROUPS = len(POOL_WINDOWS)
POOL_WIDTH = D_MODEL // 4
POOL_GROUP = POOL_WIDTH // N_POOL_GROUPS
WINDOW = 128
B_HEADS = (D_MODEL - POOL_WIDTH) // HEAD_DIM
B_KV_HEADS = B_HEADS // 3
C_HEADS = D_MODEL // HEAD_DIM
C_KV_HEADS = C_HEADS // 4
MIX_WIDTH = D_MODEL
IN_WIDTH_AB = POOL_WIDTH + (B_HEADS + 2 * B_KV_HEADS) * HEAD_DIM
IN_WIDTH_C = (C_HEADS + 2 * C_KV_HEADS) * HEAD_DIM
ROT_AXIS = HEAD_DIM // 2
ROPE_THETA = 10000.0
N_EXPERTS = 16
EXPERT_FF = D_MODEL
CAPACITY_FACTOR = 2
N_AB_LAYERS = (DEPTH + 1) // 2
N_C_LAYERS = DEPTH // 2

kernel_name = 'hybrid_pool_swa_qknorm_ec_moe_diffusion'


def rms_norm(x, g):
    xf = x.astype(jnp.float32)
    y = xf * lax.rsqrt(jnp.mean(xf * xf, axis=-1, keepdims=True) + EPS)
    return (y * g.astype(jnp.float32)).astype(x.dtype)


def adaln_params(cond, w, b):
    m = jax.nn.silu(cond) @ w + b
    return [t[:, None, :] for t in jnp.split(m, 6, axis=-1)]


def modulate(h, shift, scale):
    return h * (1 + scale) + shift


def heads(z, n):
    return z.reshape(z.shape[0], z.shape[1], n, HEAD_DIM)


def axial_rope(n_tokens):
    rows = n_tokens // GRID_W
    row = jnp.broadcast_to(jnp.arange(rows)[:, None], (rows, GRID_W)).reshape(-1)
    col = jnp.broadcast_to(jnp.arange(GRID_W)[None, :], (rows, GRID_W)).reshape(-1)
    inv = ROPE_THETA ** (-jnp.arange(0, ROT_AXIS, 2, dtype=jnp.float32) / ROT_AXIS)
    ang = jnp.stack([row.astype(jnp.float32)[:, None] * inv,
                     col.astype(jnp.float32)[:, None] * inv], axis=1)
    return jnp.cos(ang), jnp.sin(ang)


def apply_axial_rope(x, cos, sin):
    B, T, H, dh = x.shape
    xh = x.astype(jnp.float32).reshape(B, T, H, 2, 2, ROT_AXIS // 2)
    x1, x2 = xh[..., 0, :], xh[..., 1, :]
    cs, sn = cos[:, None], sin[:, None]
    out = jnp.stack([x1 * cs - x2 * sn, x2 * cs + x1 * sn], axis=-2)
    return out.reshape(B, T, H, dh).astype(x.dtype)


def multiscale_pool(u, pool_w, pool_scale):
    B, T, _ = u.shape
    uf = u.astype(jnp.float32).reshape(B, T, N_POOL_GROUPS, POOL_GROUP)
    cs = jnp.pad(jnp.cumsum(uf, axis=1), ((0, 0), (1, 0), (0, 0), (0, 0)))
    t = jnp.arange(T)
    means = []
    for g, w in enumerate(POOL_WINDOWS):
        lo = jnp.maximum(t - w // 2, 0)
        hi = jnp.minimum(t + w // 2, T)
        s = cs[:, hi, g] - cs[:, lo, g]
        means.append(s / (hi - lo).astype(jnp.float32)[None, :, None])
    d = (jnp.stack(means, axis=2) - uf).astype(u.dtype)
    y = jnp.einsum('btgc,gcd->btgd', d, pool_w) * pool_scale.reshape(N_POOL_GROUPS, POOL_GROUP)
    return y.reshape(B, T, POOL_WIDTH)


def block_attention(q, k, v, sink=None):
    B, T, Hq, dh = q.shape
    Hkv = k.shape[2]
    G = Hq // Hkv
    nb = T // BLOCK
    qb = q.reshape(B, nb, BLOCK, Hkv, G, dh).transpose(1, 0, 2, 3, 4, 5)

    def one(qblk):
        s = jnp.einsum('bqhgd,bkhd->bhgqk', qblk, k).astype(jnp.float32) * ATTN_SCALE
        if sink is not None:
            sk = jnp.broadcast_to(sink.astype(jnp.float32).reshape(1, Hkv, G, 1, 1), s.shape[:-1] + (1,))
            p = jax.nn.softmax(jnp.concatenate([s, sk], axis=-1), axis=-1)[..., :-1]
        else:
            p = jax.nn.softmax(s, axis=-1)
        return jnp.einsum('bhgqk,bkhd->bqhgd', p.astype(v.dtype), v)

    o = lax.map(one, qb)
    return o.transpose(1, 0, 2, 3, 4, 5).reshape(B, T, Hq * dh)


def windowed_attention(q, k, v, k_ctx, v_ctx, sink):
    B, S, Hq, dh = q.shape
    Hkv = k.shape[2]
    G = Hq // Hkv
    nb = S // BLOCK
    pad = ((0, 0), (BLOCK, BLOCK), (0, 0), (0, 0))
    kp, vp = jnp.pad(k, pad), jnp.pad(v, pad)
    qb = q.reshape(B, nb, BLOCK, Hkv, G, dh).transpose(1, 0, 2, 3, 4, 5)
    qi = jnp.arange(BLOCK)[:, None]
    kk = jnp.arange(3 * BLOCK)
    in_band = jnp.abs(BLOCK + qi - kk[None, :]) <= WINDOW
    sk = jnp.broadcast_to(sink.astype(jnp.float32).reshape(1, Hkv, G, 1, 1), (B, Hkv, G, BLOCK, 1))

    def one(args):
        n, qblk = args
        kw = lax.dynamic_slice_in_dim(kp, n * BLOCK, 3 * BLOCK, axis=1)
        vw = lax.dynamic_slice_in_dim(vp, n * BLOCK, 3 * BLOCK, axis=1)
        j = (n - 1) * BLOCK + kk
        mask = in_band & ((j >= 0) & (j < S))[None, :]
        s_loc = jnp.einsum('bqhgd,bkhd->bhgqk', qblk, kw).astype(jnp.float32) * ATTN_SCALE
        s_loc = jnp.where(mask, s_loc, NEG_INF)
        s_ctx = jnp.einsum('bqhgd,bkhd->bhgqk', qblk, k_ctx).astype(jnp.float32) * ATTN_SCALE
        p = jax.nn.softmax(jnp.concatenate([s_loc, s_ctx, sk], axis=-1), axis=-1)
        p_loc = p[..., :3 * BLOCK].astype(v.dtype)
        p_ctx = p[..., 3 * BLOCK:-1].astype(v.dtype)
        return (jnp.einsum('bhgqk,bkhd->bqhgd', p_loc, vw)
                + jnp.einsum('bhgqk,bkhd->bqhgd', p_ctx, v_ctx))

    o = lax.map(one, (jnp.arange(nb), qb))
    return o.transpose(1, 0, 2, 3, 4, 5).reshape(B, S, Hq * dh)


def pool_window_mixer(h, h_ctx, w_in, pool_w, pool_scale, sink, cos, sin, need_ctx):
    splits = [POOL_WIDTH, POOL_WIDTH + B_HEADS * HEAD_DIM,
              POOL_WIDTH + (B_HEADS + B_KV_HEADS) * HEAD_DIM]
    u, q, k, v = jnp.split(h @ w_in, splits, axis=-1)
    uc, qc, kc, vc = jnp.split(h_ctx @ w_in, splits, axis=-1)
    q = apply_axial_rope(heads(q, B_HEADS), cos, sin)
    k = apply_axial_rope(heads(k, B_KV_HEADS), cos, sin)
    v = heads(v, B_KV_HEADS)
    kc, vc = heads(kc, B_KV_HEADS), heads(vc, B_KV_HEADS)
    y = jnp.concatenate([multiscale_pool(u, pool_w, pool_scale),
                         windowed_attention(q, k, v, kc, vc, sink)], axis=-1)
    yc = None
    if need_ctx:
        yc = jnp.concatenate([multiscale_pool(uc, pool_w, pool_scale),
                              block_attention(heads(qc, B_HEADS), kc, vc, sink)], axis=-1)
    return y, yc


def qknorm_rope_mixer(h, h_ctx, w_in, q_g, k_g, cos, sin, need_ctx):
    splits = [C_HEADS * HEAD_DIM, (C_HEADS + C_KV_HEADS) * HEAD_DIM]
    q, k, v = jnp.split(h @ w_in, splits, axis=-1)
    qc, kc, vc = jnp.split(h_ctx @ w_in, splits, axis=-1)
    q = apply_axial_rope(rms_norm(heads(q, C_HEADS), q_g), cos, sin)
    k = apply_axial_rope(rms_norm(heads(k, C_KV_HEADS), k_g), cos, sin)
    v = heads(v, C_KV_HEADS)
    kc = rms_norm(heads(kc, C_KV_HEADS), k_g)
    vc = heads(vc, C_KV_HEADS)
    y = block_attention(q, jnp.concatenate([kc, k], axis=1), jnp.concatenate([vc, v], axis=1))
    yc = None
    if need_ctx:
        yc = block_attention(rms_norm(heads(qc, C_HEADS), q_g), kc, vc)
    return y, yc


def expert_choice_ffn(h, router_w, w_gate, w_up, w_down):
    B, T, D = h.shape
    cap = CAPACITY_FACTOR * T // N_EXPERTS
    aff = jax.nn.softmax((h @ router_w).astype(jnp.float32), axis=-1)
    gate, idx = lax.top_k(jnp.swapaxes(aff, 1, 2), cap)
    xs = jax.vmap(lambda hb, ib: hb[ib])(h, idx)
    a = jnp.einsum('becd,edf->becf', xs, w_gate)
    u = jnp.einsum('becd,edf->becf', xs, w_up)
    ye = jnp.einsum('becf,efd->becd', jax.nn.silu(a) * u, w_down) * gate[..., None].astype(h.dtype)
    return jax.vmap(lambda ib, yb: jnp.zeros((T, D), yb.dtype).at[ib.reshape(-1)].add(yb.reshape(-1, D)))(idx, ye)


def setup_inputs(seed: int = 0) -> dict:
    key = jax.random.key(seed)
    ks = jax.random.split(key, 20)

    def nrm(k, shape, scale):
        return jax.random.normal(k, shape, jnp.float32) * scale

    D = D_MODEL
    return {
        'x': nrm(ks[0], (BATCH, SEQ, D), 1.0),
        'c': nrm(ks[1], (BATCH, D), 1.0),
        'ctx': nrm(ks[2], (BATCH, CTX_LEN, D), 1.0),
        'c_ctx': nrm(ks[3], (D,), 1.0),
        'mod_w': nrm(ks[4], (DEPTH, D, 6 * D), D ** -0.5),
        'mod_b': nrm(ks[5], (DEPTH, 6 * D), 0.02),
        'norm_g': 1.0 + nrm(ks[6], (DEPTH, 2, D), 0.02),
        'w_out': nrm(ks[7], (DEPTH, MIX_WIDTH, D), MIX_WIDTH ** -0.5),
        'router_w': nrm(ks[8], (DEPTH, D, N_EXPERTS), D ** -0.5),
        'expert_w_gate': nrm(ks[9], (DEPTH, N_EXPERTS, D, EXPERT_FF), D ** -0.5),
        'expert_w_up': nrm(ks[10], (DEPTH, N_EXPERTS, D, EXPERT_FF), D ** -0.5),
        'expert_w_down': nrm(ks[11], (DEPTH, N_EXPERTS, EXPERT_FF, D), EXPERT_FF ** -0.5),
        'ab_w_in': nrm(ks[12], (N_AB_LAYERS, D, IN_WIDTH_AB), D ** -0.5),
        'pool_w': nrm(ks[13], (N_AB_LAYERS, N_POOL_GROUPS, POOL_GROUP, POOL_GROUP), POOL_GROUP ** -0.5),
        'pool_scale': 1.0 + nrm(ks[14], (N_AB_LAYERS, POOL_WIDTH), 0.1),
        'ab_sink': nrm(ks[15], (N_AB_LAYERS, B_HEADS), 0.5),
        'c_w_in': nrm(ks[16], (N_C_LAYERS, D, IN_WIDTH_C), D ** -0.5),
        'c_q_norm_g': 1.0 + nrm(ks[17], (N_C_LAYERS, HEAD_DIM), 0.02),
        'c_k_norm_g': 1.0 + nrm(ks[18], (N_C_LAYERS, HEAD_DIM), 0.02),
        'final_norm_g': 1.0 + nrm(ks[19], (D,), 0.02),
    }


def reference(x, c, ctx, c_ctx, mod_w, mod_b, norm_g, w_out, router_w, expert_w_gate,
              expert_w_up, expert_w_down, ab_w_in, pool_w, pool_scale, ab_sink, c_w_in,
              c_q_norm_g, c_k_norm_g, final_norm_g):
    cos, sin = axial_rope(x.shape[1])
    for i in range(DEPTH):
        last = i == DEPTH - 1
        j = i // 2
        sh1, sc1, g1, sh2, sc2, g2 = adaln_params(c, mod_w[i], mod_b[i])
        sh1c, sc1c, g1c, sh2c, sc2c, g2c = adaln_params(c_ctx[None], mod_w[i], mod_b[i])
        h = modulate(rms_norm(x, norm_g[i, 0]), sh1, sc1)
        hc = modulate(rms_norm(ctx, norm_g[i, 0]), sh1c, sc1c)
        if i % 2 == 0:
            y, yc = pool_window_mixer(h, hc, ab_w_in[j], pool_w[j], pool_scale[j], ab_sink[j],
                                      cos, sin, not last)
        else:
            y, yc = qknorm_rope_mixer(h, hc, c_w_in[j], c_q_norm_g[j], c_k_norm_g[j],
                                      cos, sin, not last)
        x = x + g1 * (y @ w_out[i])
        x = x + g2 * expert_choice_ffn(modulate(rms_norm(x, norm_g[i, 1]), sh2, sc2),
                                       router_w[i], expert_w_gate[i], expert_w_up[i], expert_w_down[i])
        if not last:
            ctx = ctx + g1c * (yc @ w_out[i])
            ctx = ctx + g2c * expert_choice_ffn(modulate(rms_norm(ctx, norm_g[i, 1]), sh2c, sc2c),
                                                router_w[i], expert_w_gate[i], expert_w_up[i],
                                                expert_w_down[i])
    return rms_norm(x, final_norm_g)
```

```python
import functools

import jax
import jax.numpy as jnp
from jax import lax
from jax.experimental import pallas as pl
from jax.experimental.pallas import tpu as pltpu

F32 = jnp.float32
BF16 = jnp.bfloat16

D_MODEL = 2048
DEPTH = 2
GRID_W = 64
HEAD_DIM = 128
EPS = 1e-6
NEG_INF = -1e30
ATTN_SCALE = HEAD_DIM ** -0.5
POOL_WINDOWS = (2, 4, 8, 16)
POOL_GROUP = 128
POOL_WIDTH = 512
WINDOW = 128
B_HEADS = 12
B_KV_HEADS = 4
C_HEADS = 16
C_KV_HEADS = 4
KV_WIDTH = 4 * HEAD_DIM
ROT_AXIS = HEAD_DIM // 2
ROPE_THETA = 10000.0
N_EXPERTS = 16
CAPACITY_FACTOR = 2
COND_ROWS = 16
VMEM_LIMIT = 56 * 1024 * 1024


def _cparams(*sem):
    return pltpu.CompilerParams(dimension_semantics=sem, vmem_limit_bytes=VMEM_LIMIT)


def _adaln_kernel(cond_ref, w_ref, b_ref, o_ref):
    cnd = cond_ref[...]
    s = cnd * (1.0 / (1.0 + jnp.exp(-cnd)))
    o_ref[0] = jnp.dot(s.astype(BF16), w_ref[0].astype(BF16), preferred_element_type=F32) + b_ref[0]


def _adaln(cond, mod_w, mod_b):
    tn = 1024
    six_d = mod_w.shape[-1]
    return pl.pallas_call(
        _adaln_kernel,
        grid=(DEPTH, six_d // tn),
        in_specs=[
            pl.BlockSpec((COND_ROWS, D_MODEL), lambda i, j: (0, 0)),
            pl.BlockSpec((1, D_MODEL, tn), lambda i, j: (i, 0, j)),
            pl.BlockSpec((1, 1, tn), lambda i, j: (i, 0, j)),
        ],
        out_specs=pl.BlockSpec((1, COND_ROWS, tn), lambda i, j: (i, 0, j)),
        out_shape=jax.ShapeDtypeStruct((DEPTH, COND_ROWS, six_d), F32),
        compiler_params=_cparams("parallel", "parallel"),
        name="adaln",
    )(cond, mod_w, mod_b.reshape(DEPTH, 1, six_d))


def _norm_mod(x, g, shift, scale):
    ms = jnp.mean(x * x, axis=-1, keepdims=True)
    h = x * lax.rsqrt(ms + EPS) * g
    return h * (1.0 + scale) + shift


def _rope(a, cos, sin_signed):
    lane = lax.broadcasted_iota(jnp.int32, a.shape, 1)
    first = (lane & (ROT_AXIS - 1)) < (ROT_AXIS // 2)
    partner = jnp.where(first, pltpu.roll(a, HEAD_DIM - ROT_AXIS // 2, axis=1),
                        pltpu.roll(a, ROT_AXIS // 2, axis=1))
    return a * cos + partner * sin_signed


def _inproj_kernel(*refs, n_pool, n_q, qk_norm, rope):
    x_ref, g_ref, sh_ref, sc_ref, w_ref = refs[:5]
    pos = 5
    if rope:
        cos_ref, sin_ref = refs[pos:pos + 2]
        pos += 2
    if qk_norm:
        qg_ref, kg_ref = refs[pos:pos + 2]
        pos += 2
    outs = refs[pos:]
    if n_pool:
        u_ref, q_ref, k_ref, v_ref = outs
    else:
        q_ref, k_ref, v_ref = outs

    hb = _norm_mod(x_ref[0], g_ref[...], sh_ref[0], sc_ref[0]).astype(BF16)
    n_heads = w_ref.shape[1] // HEAD_DIM
    group = 4
    for c0 in range(0, n_heads, group):
        acc = jnp.dot(hb, w_ref[:, c0 * HEAD_DIM:(c0 + group) * HEAD_DIM], preferred_element_type=F32)
        for j in range(group):
            c = c0 + j
            a = acc[:, j * HEAD_DIM:(j + 1) * HEAD_DIM]
            if c < n_pool:
                u_ref[0, :, c * HEAD_DIM:(c + 1) * HEAD_DIM] = a
                continue
            c -= n_pool
            is_q = c < n_q
            is_k = (not is_q) and c < n_q + 4
            if is_q or is_k:
                if qk_norm:
                    gain = qg_ref[...] if is_q else kg_ref[...]
                    a = a * lax.rsqrt(jnp.mean(a * a, axis=-1, keepdims=True) + EPS) * gain
                if rope:
                    a = _rope(a, cos_ref[...], sin_ref[...])
            if is_q:
                q_ref[0, :, c * HEAD_DIM:(c + 1) * HEAD_DIM] = (a * ATTN_SCALE).astype(BF16)
            elif is_k:
                c -= n_q
                k_ref[0, :, c * HEAD_DIM:(c + 1) * HEAD_DIM] = a.astype(BF16)
            else:
                c -= n_q + 4
                v_ref[0, :, c * HEAD_DIM:(c + 1) * HEAD_DIM] = a.astype(BF16)


def _inproj(x, g, shift, scale, w, *, n_pool, n_q, tm, tables=None, qk_gains=None):
    B, T, D = x.shape
    per_batch = shift.shape[0] == B
    mod_map = (lambda b, t: (b, 0, 0)) if per_batch else (lambda b, t: (0, 0, 0))
    in_specs = [
        pl.BlockSpec((1, tm, D), lambda b, t: (b, t, 0)),
        pl.BlockSpec((1, D), lambda b, t: (0, 0)),
        pl.BlockSpec((1, 1, D), mod_map),
        pl.BlockSpec((1, 1, D), mod_map),
        pl.BlockSpec(w.shape, lambda b, t: (0, 0)),
    ]
    args = [x, g.reshape(1, D), shift, scale, w]
    if tables is not None:
        in_specs += [pl.BlockSpec((tm, HEAD_DIM), lambda b, t: (t, 0))] * 2
        args += list(tables)
    if qk_gains is not None:
        in_specs += [pl.BlockSpec((1, HEAD_DIM), lambda b, t: (0, 0))] * 2
        args += [qk_gains[0].reshape(1, HEAD_DIM), qk_gains[1].reshape(1, HEAD_DIM)]
    widths = ([(n_pool * HEAD_DIM, F32)] if n_pool else []) + [
        (n_q * HEAD_DIM, BF16), (KV_WIDTH, BF16), (KV_WIDTH, BF16)]
    out_specs = [pl.BlockSpec((1, tm, wd), lambda b, t: (b, t, 0)) for wd, _ in widths]
    out_shape = [jax.ShapeDtypeStruct((B, T, wd), dt) for wd, dt in widths]
    return pl.pallas_call(
        functools.partial(_inproj_kernel, n_pool=n_pool, n_q=n_q,
                          qk_norm=qk_gains is not None, rope=tables is not None),
        grid=(B, T // tm),
        in_specs=in_specs, out_specs=out_specs, out_shape=out_shape,
        compiler_params=_cparams("parallel", "parallel"),
        name="inproj",
    )(*args)


def _pool_kernel(u_ref, w_ref, s_ref, o_ref, pad_ref, *, T, chunk):
    halo = 16
    pad_ref[0:halo, :] = jnp.zeros((halo, POOL_WIDTH), F32)
    pad_ref[halo + T:2 * halo + T, :] = jnp.zeros((halo, POOL_WIDTH), F32)
    pad_ref[halo:halo + T, :] = u_ref[0]
    for g, wdw in enumerate(POOL_WINDOWS):
        cols = slice(g * POOL_GROUP, (g + 1) * POOL_GROUP)
        half = wdw // 2
        for r0 in range(0, T, chunk):
            s = pad_ref[halo + r0 - half:halo + r0 - half + chunk, cols]
            for j in range(-half + 1, half):
                s = s + pad_ref[halo + r0 + j:halo + r0 + j + chunk, cols]
            t = r0 + lax.broadcasted_iota(jnp.int32, (chunk, 1), 0)
            cnt = (jnp.minimum(t + half, T) - jnp.maximum(t - half, 0)).astype(F32)
            d = (s / cnt - u_ref[0, r0:r0 + chunk, cols]).astype(BF16)
            y = jnp.dot(d, w_ref[g], preferred_element_type=F32) * s_ref[:, cols]
            o_ref[0, r0:r0 + chunk, cols] = y.astype(BF16)


def _pool(u, pool_w, pool_scale):
    B, T, _ = u.shape
    return pl.pallas_call(
        functools.partial(_pool_kernel, T=T, chunk=min(T, 512)),
        grid=(B,),
        in_specs=[
            pl.BlockSpec((1, T, POOL_WIDTH), lambda b: (b, 0, 0)),
            pl.BlockSpec(pool_w.shape, lambda b: (0, 0, 0)),
            pl.BlockSpec((1, POOL_WIDTH), lambda b: (0, 0)),
        ],
        out_specs=pl.BlockSpec((1, T, POOL_WIDTH), lambda b: (b, 0, 0)),
        out_shape=jax.ShapeDtypeStruct((B, T, POOL_WIDTH), BF16),
        scratch_shapes=[pltpu.VMEM((T + 32, POOL_WIDTH), F32)],
        compiler_params=_cparams("parallel"),
        name="pool",
    )(u, pool_w, pool_scale.reshape(1, POOL_WIDTH))


def _stack_heads(q_ref, h, G):
    return jnp.concatenate(
        [q_ref[0, :, (h * G + g) * HEAD_DIM:(h * G + g + 1) * HEAD_DIM] for g in range(G)], axis=0)


def _qk(q, k):
    return lax.dot_general(q, k, (((1,), (1,)), ((), ())), preferred_element_type=F32)


def _sink_column(sink_ref, h, G, tq):
    return jnp.concatenate([jnp.full((tq, 1), sink_ref[h * G + g], F32) for g in range(G)], axis=0)


def _win_attn_kernel(sink_ref, q_ref, k_ref, v_ref, kc_ref, vc_ref, o_ref, *, T, tq):
    G = B_HEADS // B_KV_HEADS
    kw_len = tq + 2 * WINDOW
    q0 = pl.program_id(1) * tq
    start = pl.multiple_of(jnp.clip(q0 - WINDOW, 0, T - kw_len), WINDOW)
    delta = q0 - start
    r = lax.broadcasted_iota(jnp.int32, (G * tq, kw_len), 0) & (tq - 1)
    c = lax.broadcasted_iota(jnp.int32, (G * tq, kw_len), 1)
    band = jnp.abs(r + delta - c) <= WINDOW
    for h in range(B_KV_HEADS):
        hs = slice(h * HEAD_DIM, (h + 1) * HEAD_DIM)
        qs = _stack_heads(q_ref, h, G)
        s_loc = jnp.where(band, _qk(qs, k_ref[0, pl.ds(start, kw_len), hs]), NEG_INF)
        s_ctx = _qk(qs, kc_ref[0, :, hs])
        sink = _sink_column(sink_ref, h, G, tq)
        m = jnp.maximum(jnp.maximum(jnp.max(s_loc, axis=-1, keepdims=True),
                                    jnp.max(s_ctx, axis=-1, keepdims=True)), sink)
        p_loc = jnp.exp(s_loc - m)
        p_ctx = jnp.exp(s_ctx - m)
        den = (jnp.sum(p_loc, axis=-1, keepdims=True) + jnp.sum(p_ctx, axis=-1, keepdims=True)
               + jnp.exp(sink - m))
        o = (jnp.dot(p_loc.astype(BF16), v_ref[0, pl.ds(start, kw_len), hs], preferred_element_type=F32)
             + jnp.dot(p_ctx.astype(BF16), vc_ref[0, :, hs], preferred_element_type=F32)) / den
        for g in range(G):
            o_ref[0, :, (h * G + g) * HEAD_DIM:(h * G + g + 1) * HEAD_DIM] = (
                o[g * tq:(g + 1) * tq].astype(BF16))


def _win_attn(sink, q, k, v, kc, vc, *, tq=128):
    B, T, qw = q.shape
    Tc = kc.shape[1]
    return pl.pallas_call(
        functools.partial(_win_attn_kernel, T=T, tq=tq),
        grid=(B, T // tq),
        in_specs=[
            pl.BlockSpec(memory_space=pltpu.SMEM),
            pl.BlockSpec((1, tq, qw), lambda b, t: (b, t, 0)),
            pl.BlockSpec((1, T, KV_WIDTH), lambda b, t: (b, 0, 0)),
            pl.BlockSpec((1, T, KV_WIDTH), lambda b, t: (b, 0, 0)),
            pl.BlockSpec((1, Tc, KV_WIDTH), lambda b, t: (b, 0, 0)),
            pl.BlockSpec((1, Tc, KV_WIDTH), lambda b, t: (b, 0, 0)),
        ],
        out_specs=pl.BlockSpec((1, tq, qw), lambda b, t: (b, t, 0)),
        out_shape=jax.ShapeDtypeStruct((B, T, qw), BF16),
        compiler_params=_cparams("parallel", "arbitrary"),
        name="win_attn",
    )(sink, q, k, v, kc, vc)


def _flash_update(qs, k, v, m, l, acc):
    s = _qk(qs, k)
    m_new = jnp.maximum(m, jnp.max(s, axis=-1, keepdims=True))
    alpha = jnp.exp(m - m_new)
    p = jnp.exp(s - m_new)
    l = alpha * l + jnp.sum(p, axis=-1, keepdims=True)
    acc = alpha * acc + jnp.dot(p.astype(BF16), v, preferred_element_type=F32)
    return m_new, l, acc


def _attn_kernel(*refs, G, n_src, tk, has_sink):
    pos = 0
    if has_sink:
        sink_ref = refs[0]
        pos = 1
    q_ref = refs[pos]
    kv_refs = refs[pos + 1:pos + 1 + 2 * n_src]
    o_ref = refs[pos + 1 + 2 * n_src]
    tq = q_ref.shape[1]
    n_kv = q_ref.shape[2] // (G * HEAD_DIM)
    for h in range(n_kv):
        hs = slice(h * HEAD_DIM, (h + 1) * HEAD_DIM)
        qs = _stack_heads(q_ref, h, G)
        if has_sink:
            m = _sink_column(sink_ref, h, G, tq)
            l = jnp.ones((G * tq, 1), F32)
        else:
            m = jnp.full((G * tq, 1), NEG_INF, F32)
            l = jnp.zeros((G * tq, 1), F32)
        acc = jnp.zeros((G * tq, HEAD_DIM), F32)
        for s_i in range(n_src):
            k_ref, v_ref = kv_refs[2 * s_i], kv_refs[2 * s_i + 1]
            n_chunks = k_ref.shape[1] // tk
            if n_chunks == 1:
                m, l, acc = _flash_update(qs, k_ref[0, :, hs], v_ref[0, :, hs], m, l, acc)
            else:
                def body(ci, carry, k_ref=k_ref, v_ref=v_ref):
                    rows = pl.ds(pl.multiple_of(ci * tk, tk), tk)
                    return _flash_update(qs, k_ref[0, rows, hs], v_ref[0, rows, hs], *carry)
                m, l, acc = lax.fori_loop(0, n_chunks, body, (m, l, acc))
        o = acc / l
        for g in range(G):
            o_ref[0, :, (h * G + g) * HEAD_DIM:(h * G + g + 1) * HEAD_DIM] = (
                o[g * tq:(g + 1) * tq].astype(BF16))


def _attn(q, kvs, *, G, tq, tk, sink=None):
    B, T, qw = q.shape
    in_specs, args = [], []
    if sink is not None:
        in_specs.append(pl.BlockSpec(memory_space=pltpu.SMEM))
        args.append(sink)
    in_specs.append(pl.BlockSpec((1, tq, qw), lambda b, t: (b, t, 0)))
    args.append(q)
    for a in kvs:
        in_specs.append(pl.BlockSpec((1, a.shape[1], KV_WIDTH), lambda b, t: (b, 0, 0)))
        args.append(a)
    return pl.pallas_call(
        functools.partial(_attn_kernel, G=G, n_src=len(kvs) // 2, tk=tk, has_sink=sink is not None),
        grid=(B, T // tq),
        in_specs=in_specs,
        out_specs=pl.BlockSpec((1, tq, qw), lambda b, t: (b, t, 0)),
        out_shape=jax.ShapeDtypeStruct((B, T, qw), BF16),
        compiler_params=_cparams("parallel", "arbitrary"),
        name="attn",
    )(*args)


def _split3(a):
    hi = a.astype(BF16)
    r1 = a - hi.astype(F32)
    mid = r1.astype(BF16)
    lo = (r1 - mid.astype(F32)).astype(BF16)
    return hi, mid, lo


def _outproj_kernel(*refs, n_y):
    y_refs = refs[:n_y]
    (w_ref, x_ref, g1_ref, ng_ref, sh_ref, sc_ref, rw_ref, xo_ref, h_ref, lg_ref) = refs[n_y:]
    acc = None
    row = 0
    for y_ref in y_refs:
        wd = y_ref.shape[2]
        part = jnp.dot(y_ref[0], w_ref[row:row + wd, :], preferred_element_type=F32)
        acc = part if acc is None else acc + part
        row += wd
    x = x_ref[0] + g1_ref[0] * acc
    xo_ref[0] = x
    h = _norm_mod(x, ng_ref[...], sh_ref[0], sc_ref[0])
    h_ref[0] = h.astype(BF16)
    h_hi, h_mid, h_lo = _split3(h)
    w_hi, w_mid, w_lo = rw_ref[0], rw_ref[1], rw_ref[2]
    dot = functools.partial(jnp.dot, preferred_element_type=F32)
    lg_ref[0] = (dot(h_hi, w_hi) + (dot(h_hi, w_mid) + dot(h_mid, w_hi))
                 + (dot(h_hi, w_lo) + dot(h_mid, w_mid) + dot(h_lo, w_hi)))


def _outproj(ys, w, x, g1, ng, shift, scale, rw3, *, tm):
    B, T, D = x.shape
    per_batch = g1.shape[0] == B
    mod_map = (lambda b, t: (b, 0, 0)) if per_batch else (lambda b, t: (0, 0, 0))
    tile = lambda wd: pl.BlockSpec((1, tm, wd), lambda b, t: (b, t, 0))
    in_specs = [tile(y.shape[2]) for y in ys] + [
        pl.BlockSpec(w.shape, lambda b, t: (0, 0)),
        tile(D),
        pl.BlockSpec((1, 1, D), mod_map),
        pl.BlockSpec((1, D), lambda b, t: (0, 0)),
        pl.BlockSpec((1, 1, D), mod_map),
        pl.BlockSpec((1, 1, D), mod_map),
        pl.BlockSpec(rw3.shape, lambda b, t: (0, 0, 0)),
    ]
    return pl.pallas_call(
        functools.partial(_outproj_kernel, n_y=len(ys)),
        grid=(B, T // tm),
        in_specs=in_specs,
        out_specs=[tile(D), tile(D), tile(N_EXPERTS)],
        out_shape=[jax.ShapeDtypeStruct((B, T, D), F32),
                   jax.ShapeDtypeStruct((B, T, D), BF16),
                   jax.ShapeDtypeStruct((B, T, N_EXPERTS), F32)],
        compiler_params=_cparams("parallel", "parallel"),
        name="outproj",
    )(*ys, w, x, g1, ng.reshape(1, D), shift, scale, rw3)


def _expert_kernel(xs_ref, gate_ref, wg_ref, wu_ref, wd_ref, o_ref, acc_ref):
    f = pl.program_id(2)
    xs = xs_ref[0]
    a = jnp.dot(xs, wg_ref[0], preferred_element_type=F32)
    u = jnp.dot(xs, wu_ref[0], preferred_element_type=F32)
    act = (a * (1.0 / (1.0 + jnp.exp(-a))) * u).astype(BF16)
    part = jnp.dot(act, wd_ref[0], preferred_element_type=F32)

    @pl.when(f == 0)
    def _():
        acc_ref[...] = part

    @pl.when(f > 0)
    def _():
        acc_ref[...] += part

    @pl.when(f == pl.num_programs(2) - 1)
    def _():
        o_ref[0] = acc_ref[...] * gate_ref[0]


def _experts(xs, gate, wg, wu, wd, *, tf=512):
    E, M, D = xs.shape
    F = wg.shape[2]
    tm = M // 4
    return pl.pallas_call(
        _expert_kernel,
        grid=(E, M // tm, F // tf),
        in_specs=[
            pl.BlockSpec((1, tm, D), lambda e, m, f: (e, m, 0)),
            pl.BlockSpec((1, tm, 1), lambda e, m, f: (e, m, 0)),
            pl.BlockSpec((1, D, tf), lambda e, m, f: (e, 0, f)),
            pl.BlockSpec((1, D, tf), lambda e, m, f: (e, 0, f)),
            pl.BlockSpec((1, tf, D), lambda e, m, f: (e, f, 0)),
        ],
        out_specs=pl.BlockSpec((1, tm, D), lambda e, m, f: (e, m, 0)),
        out_shape=jax.ShapeDtypeStruct((E, M, D), F32),
        scratch_shapes=[pltpu.VMEM((tm, D), F32)],
        compiler_params=_cparams("parallel", "parallel", "arbitrary"),
        name="experts",
    )(xs, gate, wg, wu, wd)


def _final_norm_kernel(x_ref, g_ref, o_ref):
    x = x_ref[0]
    o_ref[0] = x * lax.rsqrt(jnp.mean(x * x, axis=-1, keepdims=True) + EPS) * g_ref[...]


def _final_norm(x, g, *, tm=512):
    B, T, D = x.shape
    return pl.pallas_call(
        _final_norm_kernel,
        grid=(B, T // tm),
        in_specs=[pl.BlockSpec((1, tm, D), lambda b, t: (b, t, 0)),
                  pl.BlockSpec((1, D), lambda b, t: (0, 0))],
        out_specs=pl.BlockSpec((1, tm, D), lambda b, t: (b, t, 0)),
        out_shape=jax.ShapeDtypeStruct((B, T, D), F32),
        compiler_params=_cparams("parallel", "parallel"),
        name="final_norm",
    )(x, g.reshape(1, D))


def _rope_tables(n_tokens):
    rows = n_tokens // GRID_W
    row = jnp.broadcast_to(jnp.arange(rows)[:, None], (rows, GRID_W)).reshape(-1).astype(F32)
    col = jnp.broadcast_to(jnp.arange(GRID_W)[None, :], (rows, GRID_W)).reshape(-1).astype(F32)
    inv = ROPE_THETA ** (-jnp.arange(0, ROT_AXIS, 2, dtype=F32) / ROT_AXIS)
    ang_r, ang_c = row[:, None] * inv, col[:, None] * inv
    cos = jnp.concatenate([jnp.cos(ang_r)] * 2 + [jnp.cos(ang_c)] * 2, axis=1)
    sin = jnp.concatenate([-jnp.sin(ang_r), jnp.sin(ang_r), -jnp.sin(ang_c), jnp.sin(ang_c)], axis=1)
    return cos, sin


def _route(h, logits):
    B, T, D = h.shape
    cap = CAPACITY_FACTOR * T // N_EXPERTS
    aff = jax.nn.softmax(logits, axis=-1)
    gate, idx = lax.top_k(jnp.swapaxes(aff, 1, 2), cap)
    xs = jax.vmap(lambda hb, ib: hb[ib])(h, idx)
    xs = jnp.swapaxes(xs, 0, 1).reshape(N_EXPERTS, B * cap, D)
    gate = jnp.swapaxes(gate, 0, 1).reshape(N_EXPERTS, B * cap, 1)
    return xs, gate, idx


def _unroute(ye, idx, T):
    E, M, D = ye.shape
    B, _, cap = idx.shape
    ye = jnp.swapaxes(ye.reshape(E, B, cap, D), 0, 1)
    return jax.vmap(lambda ib, yb: jnp.zeros((T, D), yb.dtype).at[ib.reshape(-1)].add(yb.reshape(-1, D)))(idx, ye)


def kernel(x, c, ctx, c_ctx, mod_w, mod_b, norm_g, w_out, router_w, expert_w_gate, expert_w_up,
           expert_w_down, ab_w_in, pool_w, pool_scale, ab_sink, c_w_in, c_q_norm_g, c_k_norm_g,
           final_norm_g):
    B, T, D = x.shape
    Tc = ctx.shape[1]
    tables = _rope_tables(T)

    cond = jnp.zeros((COND_ROWS, D), F32).at[:B].set(c).at[B].set(c_ctx)
    mod = _adaln(cond, mod_w, mod_b)

    for i in range(DEPTH):
        last = i == DEPTH - 1
        j = i // 2
        mx = [mod[i, :B, k * D:(k + 1) * D].reshape(B, 1, D) for k in range(6)]
        mc = [mod[i, B:B + 1, k * D:(k + 1) * D].reshape(1, 1, D) for k in range(6)]
        rw3 = jnp.stack(_split3(router_w[i]))
        wo = w_out[i].astype(BF16)
        wg, wu, wd = (expert_w_gate[i].astype(BF16), expert_w_up[i].astype(BF16),
                      expert_w_down[i].astype(BF16))
        if i % 2 == 0:
            w_in = ab_w_in[j].astype(BF16)
            u, q, k, v = _inproj(x, norm_g[i, 0], mx[0], mx[1], w_in, n_pool=4, n_q=B_HEADS,
                                 tm=512, tables=tables)
            uc, qc, kc, vc = _inproj(ctx, norm_g[i, 0], mc[0], mc[1], w_in, n_pool=4, n_q=B_HEADS, tm=Tc)
            pw = pool_w[j].astype(BF16)
            ys = [_pool(u, pw, pool_scale[j]), _win_attn(ab_sink[j], q, k, v, kc, vc)]
            if not last:
                ycs = [_pool(uc, pw, pool_scale[j]),
                       _attn(qc, [kc, vc], G=B_HEADS // B_KV_HEADS, tq=Tc, tk=Tc, sink=ab_sink[j])]
        else:
            w_in = c_w_in[j].astype(BF16)
            gains = (c_q_norm_g[j], c_k_norm_g[j])
            q, k, v = _inproj(x, norm_g[i, 0], mx[0], mx[1], w_in, n_pool=0, n_q=C_HEADS,
                              tm=512, tables=tables, qk_gains=gains)
            qc, kc, vc = _inproj(ctx, norm_g[i, 0], mc[0], mc[1], w_in, n_pool=0, n_q=C_HEADS,
                                 tm=Tc, qk_gains=gains)
            ys = [_attn(q, [kc, vc, k, v], G=C_HEADS // C_KV_HEADS, tq=256, tk=Tc)]
            if not last:
                ycs = [_attn(qc, [kc, vc], G=C_HEADS // C_KV_HEADS, tq=Tc, tk=Tc)]

        x, h2, lg = _outproj(ys, wo, x, mx[2], norm_g[i, 1], mx[3], mx[4], rw3, tm=512)
        xs, gate, idx = _route(h2, lg)
        if not last:
            ctx, h2c, lgc = _outproj(ycs, wo, ctx, mc[2], norm_g[i, 1], mc[3], mc[4], rw3, tm=Tc)
            xsc, gatec, idxc = _route(h2c, lgc)
            n_lat = xs.shape[1]
            ye = _experts(jnp.concatenate([xs, xsc], axis=1), jnp.concatenate([gate, gatec], axis=1),
                          wg, wu, wd)
            ctx = ctx + mc[5] * _unroute(ye[:, n_lat:], idxc, Tc)
            ye = ye[:, :n_lat]
        else:
            ye = _experts(xs, gate, wg, wu, wd)
        x = x + mx[5] * _unroute(ye, idx, T)

    return _final_norm(x, final_norm_g)
```

```python
import functools

import jax
import jax.numpy as jnp
from jax import lax
from jax.experimental import pallas as pl
from jax.experimental.pallas import tpu as pltpu

F32 = jnp.float32
BF16 = jnp.bfloat16

D_MODEL = 2048
DEPTH = 2
GRID_W = 64
HEAD_DIM = 128
EPS = 1e-6
NEG_INF = -1e30
ATTN_SCALE = HEAD_DIM ** -0.5
POOL_WINDOWS = (2, 4, 8, 16)
POOL_GROUP = 128
POOL_WIDTH = 512
WINDOW = 128
B_HEADS = 12
B_KV_HEADS = 4
C_HEADS = 16
C_KV_HEADS = 4
KV_WIDTH = 4 * HEAD_DIM
V_HEAD = 2 * HEAD_DIM
V_WIDTH = 4 * V_HEAD
LOG2E = 1.4426950408889634
Q_SCALE = ATTN_SCALE * LOG2E
ROT_AXIS = HEAD_DIM // 2
ROPE_THETA = 10000.0
N_EXPERTS = 16
CAPACITY_FACTOR = 2
COND_ROWS = 16
VMEM_LIMIT = 56 * 1024 * 1024


def _cparams(*sem):
    return pltpu.CompilerParams(dimension_semantics=sem, vmem_limit_bytes=VMEM_LIMIT)


def _adaln_kernel(cond_ref, w_ref, b_ref, o_ref):
    cnd = cond_ref[...]
    s = cnd * (1.0 / (1.0 + jnp.exp(-cnd)))
    o_ref[0] = jnp.dot(s.astype(BF16), w_ref[0].astype(BF16), preferred_element_type=F32) + b_ref[0]


def _adaln(cond, mod_w, mod_b):
    tn = 1024
    six_d = mod_w.shape[-1]
    return pl.pallas_call(
        _adaln_kernel,
        grid=(DEPTH, six_d // tn),
        in_specs=[
            pl.BlockSpec((COND_ROWS, D_MODEL), lambda i, j: (0, 0)),
            pl.BlockSpec((1, D_MODEL, tn), lambda i, j: (i, 0, j)),
            pl.BlockSpec((1, 1, tn), lambda i, j: (i, 0, j)),
        ],
        out_specs=pl.BlockSpec((1, COND_ROWS, tn), lambda i, j: (i, 0, j)),
        out_shape=jax.ShapeDtypeStruct((DEPTH, COND_ROWS, six_d), F32),
        compiler_params=_cparams("parallel", "parallel"),
        name="adaln",
    )(cond, mod_w, mod_b.reshape(DEPTH, 1, six_d))


def _norm_mod(x, g, shift, scale):
    ms = jnp.mean(x * x, axis=-1, keepdims=True)
    h = x * lax.rsqrt(ms + EPS) * g
    return h * (1.0 + scale) + shift


def _rope(a, cos, sin_signed):
    lane = lax.broadcasted_iota(jnp.int32, a.shape, 1)
    first = (lane & (ROT_AXIS - 1)) < (ROT_AXIS // 2)
    partner = jnp.where(first, pltpu.roll(a, HEAD_DIM - ROT_AXIS // 2, axis=1),
                        pltpu.roll(a, ROT_AXIS // 2, axis=1))
    return a * cos + partner * sin_signed


def _inproj_kernel(*refs, n_pool, n_q, qk_norm, rope):
    x_ref, g_ref, sh_ref, sc_ref, w_ref = refs[:5]
    pos = 5
    if rope:
        cos_ref, sin_ref = refs[pos:pos + 2]
        pos += 2
    if qk_norm:
        qg_ref, kg_ref = refs[pos:pos + 2]
        pos += 2
    outs = refs[pos:]
    if n_pool:
        u_ref, q_ref, k_ref, v_ref = outs
    else:
        q_ref, k_ref, v_ref = outs

    hb = _norm_mod(x_ref[0], g_ref[...], sh_ref[0], sc_ref[0]).astype(BF16)
    n_heads = w_ref.shape[1] // HEAD_DIM
    group = 4
    for c0 in range(0, n_heads, group):
        acc = jnp.dot(hb, w_ref[:, c0 * HEAD_DIM:(c0 + group) * HEAD_DIM], preferred_element_type=F32)
        for j in range(group):
            c = c0 + j
            a = acc[:, j * HEAD_DIM:(j + 1) * HEAD_DIM]
            if c < n_pool:
                u_ref[0, :, c * HEAD_DIM:(c + 1) * HEAD_DIM] = a
                continue
            c -= n_pool
            is_q = c < n_q
            is_k = (not is_q) and c < n_q + 4
            if is_q or is_k:
                if qk_norm:
                    gain = qg_ref[...] if is_q else kg_ref[...]
                    a = a * lax.rsqrt(jnp.mean(a * a, axis=-1, keepdims=True) + EPS) * gain
                if rope:
                    a = _rope(a, cos_ref[...], sin_ref[...])
            if is_q:
                q_ref[0, :, c * HEAD_DIM:(c + 1) * HEAD_DIM] = (a * Q_SCALE).astype(BF16)
            elif is_k:
                c -= n_q
                k_ref[0, :, c * HEAD_DIM:(c + 1) * HEAD_DIM] = a.astype(BF16)
            else:
                c -= n_q + 4
                v_ref[0, :, c * V_HEAD:c * V_HEAD + HEAD_DIM] = a.astype(BF16)
                v_ref[0, :, c * V_HEAD + HEAD_DIM:(c + 1) * V_HEAD] = jnp.ones(a.shape, BF16)


def _inproj(x, g, shift, scale, w, *, n_pool, n_q, tm, tables=None, qk_gains=None):
    B, T, D = x.shape
    per_batch = shift.shape[0] == B
    mod_map = (lambda b, t: (b, 0, 0)) if per_batch else (lambda b, t: (0, 0, 0))
    in_specs = [
        pl.BlockSpec((1, tm, D), lambda b, t: (b, t, 0)),
        pl.BlockSpec((1, D), lambda b, t: (0, 0)),
        pl.BlockSpec((1, 1, D), mod_map),
        pl.BlockSpec((1, 1, D), mod_map),
        pl.BlockSpec(w.shape, lambda b, t: (0, 0)),
    ]
    args = [x, g.reshape(1, D), shift, scale, w]
    if tables is not None:
        in_specs += [pl.BlockSpec((tm, HEAD_DIM), lambda b, t: (t, 0))] * 2
        args += list(tables)
    if qk_gains is not None:
        in_specs += [pl.BlockSpec((1, HEAD_DIM), lambda b, t: (0, 0))] * 2
        args += [qk_gains[0].reshape(1, HEAD_DIM), qk_gains[1].reshape(1, HEAD_DIM)]
    widths = ([(n_pool * HEAD_DIM, F32)] if n_pool else []) + [
        (n_q * HEAD_DIM, BF16), (KV_WIDTH, BF16), (V_WIDTH, BF16)]
    out_specs = [pl.BlockSpec((1, tm, wd), lambda b, t: (b, t, 0)) for wd, _ in widths]
    out_shape = [jax.ShapeDtypeStruct((B, T, wd), dt) for wd, dt in widths]
    return pl.pallas_call(
        functools.partial(_inproj_kernel, n_pool=n_pool, n_q=n_q,
                          qk_norm=qk_gains is not None, rope=tables is not None),
        grid=(B, T // tm),
        in_specs=in_specs, out_specs=out_specs, out_shape=out_shape,
        compiler_params=_cparams("parallel", "parallel"),
        name="inproj",
    )(*args)


def _pool_kernel(u_ref, w_ref, s_ref, o_ref, pad_ref, *, T, chunk):
    halo = 16
    pad_ref[0:halo, :] = jnp.zeros((halo, POOL_WIDTH), F32)
    pad_ref[halo + T:2 * halo + T, :] = jnp.zeros((halo, POOL_WIDTH), F32)
    pad_ref[halo:halo + T, :] = u_ref[0]
    for g, wdw in enumerate(POOL_WINDOWS):
        cols = slice(g * POOL_GROUP, (g + 1) * POOL_GROUP)
        half = wdw // 2
        for r0 in range(0, T, chunk):
            s = pad_ref[halo + r0 - half:halo + r0 - half + chunk, cols]
            for j in range(-half + 1, half):
                s = s + pad_ref[halo + r0 + j:halo + r0 + j + chunk, cols]
            t = r0 + lax.broadcasted_iota(jnp.int32, (chunk, 1), 0)
            cnt = (jnp.minimum(t + half, T) - jnp.maximum(t - half, 0)).astype(F32)
            d = (s / cnt - u_ref[0, r0:r0 + chunk, cols]).astype(BF16)
            y = jnp.dot(d, w_ref[g], preferred_element_type=F32) * s_ref[:, cols]
            o_ref[0, r0:r0 + chunk, cols] = y.astype(BF16)


def _pool(u, pool_w, pool_scale):
    B, T, _ = u.shape
    return pl.pallas_call(
        functools.partial(_pool_kernel, T=T, chunk=min(T, 512)),
        grid=(B,),
        in_specs=[
            pl.BlockSpec((1, T, POOL_WIDTH), lambda b: (b, 0, 0)),
            pl.BlockSpec(pool_w.shape, lambda b: (0, 0, 0)),
            pl.BlockSpec((1, POOL_WIDTH), lambda b: (0, 0)),
        ],
        out_specs=pl.BlockSpec((1, T, POOL_WIDTH), lambda b: (b, 0, 0)),
        out_shape=jax.ShapeDtypeStruct((B, T, POOL_WIDTH), BF16),
        scratch_shapes=[pltpu.VMEM((T + 32, POOL_WIDTH), F32)],
        compiler_params=_cparams("parallel"),
        name="pool",
    )(u, pool_w, pool_scale.reshape(1, POOL_WIDTH))


def _stack_heads(q_ref, h, G):
    return jnp.concatenate(
        [q_ref[0, :, (h * G + g) * HEAD_DIM:(h * G + g + 1) * HEAD_DIM] for g in range(G)], axis=0)


def _qk(q, k):
    return lax.dot_general(q, k, (((1,), (1,)), ((), ())), preferred_element_type=F32)


def _pv(p, v):
    return jnp.dot(p.astype(BF16), v, preferred_element_type=F32)


def _sink_column(sink_ref, h, G, tq):
    return jnp.concatenate([jnp.full((tq, 1), sink_ref[h * G + g] * LOG2E, F32) for g in range(G)], axis=0)


def _store_heads(o_ref, o, h, G, tq):
    for g in range(G):
        o_ref[0, :, (h * G + g) * HEAD_DIM:(h * G + g + 1) * HEAD_DIM] = o[g * tq:(g + 1) * tq].astype(BF16)


def _win_attn_kernel(sink_ref, q_ref, k_ref, v_ref, kc_ref, vc_ref, o_ref, *, T, tq):
    G = B_HEADS // B_KV_HEADS
    kw_len = tq + 2 * WINDOW
    q0 = pl.program_id(1) * tq
    start = pl.multiple_of(jnp.clip(q0 - WINDOW, 0, T - kw_len), WINDOW)
    delta = q0 - start
    r = lax.broadcasted_iota(jnp.int32, (G * tq, kw_len), 0) & (tq - 1)
    c = lax.broadcasted_iota(jnp.int32, (G * tq, kw_len), 1)
    band = jnp.abs(r + delta - c) <= WINDOW
    for h in range(B_KV_HEADS):
        hs = slice(h * HEAD_DIM, (h + 1) * HEAD_DIM)
        vs = slice(h * V_HEAD, (h + 1) * V_HEAD)
        qs = _stack_heads(q_ref, h, G)
        s_loc = jnp.where(band, _qk(qs, k_ref[0, pl.ds(start, kw_len), hs]), NEG_INF)
        s_ctx = _qk(qs, kc_ref[0, :, hs])
        sink = _sink_column(sink_ref, h, G, tq)
        m = jnp.maximum(jnp.maximum(jnp.max(s_loc, axis=-1, keepdims=True),
                                    jnp.max(s_ctx, axis=-1, keepdims=True)), sink)
        oa = (_pv(jnp.exp2(s_loc - m), v_ref[0, pl.ds(start, kw_len), vs])
              + _pv(jnp.exp2(s_ctx - m), vc_ref[0, :, vs]))
        o = oa[:, :HEAD_DIM] / (oa[:, HEAD_DIM:] + jnp.exp2(sink - m))
        _store_heads(o_ref, o, h, G, tq)


def _win_attn(sink, q, k, v, kc, vc, *, tq=128):
    B, T, qw = q.shape
    Tc = kc.shape[1]
    return pl.pallas_call(
        functools.partial(_win_attn_kernel, T=T, tq=tq),
        grid=(B, T // tq),
        in_specs=[
            pl.BlockSpec(memory_space=pltpu.SMEM),
            pl.BlockSpec((1, tq, qw), lambda b, t: (b, t, 0)),
            pl.BlockSpec((1, T, KV_WIDTH), lambda b, t: (b, 0, 0)),
            pl.BlockSpec((1, T, V_WIDTH), lambda b, t: (b, 0, 0)),
            pl.BlockSpec((1, Tc, KV_WIDTH), lambda b, t: (b, 0, 0)),
            pl.BlockSpec((1, Tc, V_WIDTH), lambda b, t: (b, 0, 0)),
        ],
        out_specs=pl.BlockSpec((1, tq, qw), lambda b, t: (b, t, 0)),
        out_shape=jax.ShapeDtypeStruct((B, T, qw), BF16),
        compiler_params=_cparams("parallel", "arbitrary"),
        name="win_attn",
    )(sink, q, k, v, kc, vc)


def _ctx_attn_kernel(*refs, G, has_sink):
    if has_sink:
        sink_ref, q_ref, k_ref, v_ref, o_ref = refs
    else:
        q_ref, k_ref, v_ref, o_ref = refs
    tq = q_ref.shape[1]
    for h in range(q_ref.shape[2] // (G * HEAD_DIM)):
        qs = _stack_heads(q_ref, h, G)
        s = _qk(qs, k_ref[0, :, h * HEAD_DIM:(h + 1) * HEAD_DIM])
        m = jnp.max(s, axis=-1, keepdims=True)
        if has_sink:
            sink = _sink_column(sink_ref, h, G, tq)
            m = jnp.maximum(m, sink)
        oa = _pv(jnp.exp2(s - m), v_ref[0, :, h * V_HEAD:(h + 1) * V_HEAD])
        den = oa[:, HEAD_DIM:]
        if has_sink:
            den = den + jnp.exp2(sink - m)
        _store_heads(o_ref, oa[:, :HEAD_DIM] / den, h, G, tq)


def _ctx_attn(q, k, v, *, G, sink=None):
    B, Tc, qw = q.shape
    in_specs, args = [], []
    if sink is not None:
        in_specs.append(pl.BlockSpec(memory_space=pltpu.SMEM))
        args.append(sink)
    in_specs += [pl.BlockSpec((1, Tc, a.shape[2]), lambda b: (b, 0, 0)) for a in (q, k, v)]
    return pl.pallas_call(
        functools.partial(_ctx_attn_kernel, G=G, has_sink=sink is not None),
        grid=(B,),
        in_specs=in_specs,
        out_specs=pl.BlockSpec((1, Tc, qw), lambda b: (b, 0, 0)),
        out_shape=jax.ShapeDtypeStruct((B, Tc, qw), BF16),
        compiler_params=_cparams("parallel"),
        name="ctx_attn",
    )(*args, q, k, v)


def _lane_tile_max(s):
    parts = [s[:, t * HEAD_DIM:(t + 1) * HEAD_DIM] for t in range(s.shape[1] // HEAD_DIM)]
    while len(parts) > 1:
        parts = [jnp.maximum(parts[i], parts[i + 1]) if i + 1 < len(parts) else parts[i]
                 for i in range(0, len(parts), 2)]
    return parts[0]


def _full_attn_kernel(q_ref, kc_ref, vc_ref, k_ref, v_ref, o_ref, s_ref, p_ref, m_ref, *, G, row_block, key_chunk):
    tq = q_ref.shape[1]
    R = G * tq
    half = R // 2
    Tc, Tk = kc_ref.shape[1], s_ref.shape[2]
    n_lane_tiles = Tk // HEAD_DIM
    n_heads = q_ref.shape[2] // (G * HEAD_DIM)

    def scores(h):
        hs = slice(h * HEAD_DIM, (h + 1) * HEAD_DIM)
        qs = _stack_heads(q_ref, h, G)
        for r0 in (0, half):
            q_half = qs[r0:r0 + half]
            s = _qk(q_half, kc_ref[0, :, hs])
            s_ref[h % 2, r0:r0 + half, 0:Tc] = s
            m_run = _lane_tile_max(s)
            for c0 in range(0, Tk - Tc, key_chunk):
                s = _qk(q_half, k_ref[0, c0:c0 + key_chunk, hs])
                s_ref[h % 2, r0:r0 + half, Tc + c0:Tc + c0 + key_chunk] = s
                m_run = jnp.maximum(m_run, _lane_tile_max(s))
            m_ref[h % 2, r0:r0 + half, :] = jnp.broadcast_to(
                jnp.max(m_run, axis=-1, keepdims=True), (half, HEAD_DIM))

    def weights(h):
        for r0 in range(0, R, row_block):
            mb = m_ref[h % 2, r0:r0 + row_block, :]
            for t in range(n_lane_tiles):
                cols = slice(t * HEAD_DIM, (t + 1) * HEAD_DIM)
                p_ref[h % 2, r0:r0 + row_block, cols] = jnp.exp2(
                    s_ref[h % 2, r0:r0 + row_block, cols] - mb).astype(BF16)

    def values(h):
        vs = slice(h * V_HEAD, (h + 1) * V_HEAD)
        for r0 in (0, half):
            oa = (jnp.dot(p_ref[h % 2, r0:r0 + half, 0:Tc], vc_ref[0, :, vs], preferred_element_type=F32)
                  + jnp.dot(p_ref[h % 2, r0:r0 + half, Tc:Tk], v_ref[0, :, vs], preferred_element_type=F32))
            o = oa[:, :HEAD_DIM] / oa[:, HEAD_DIM:]
            for g in range(G):
                lo, hi = g * tq, (g + 1) * tq
                a, bnd = max(lo, r0), min(hi, r0 + half)
                if a < bnd:
                    o_ref[0, a - lo:bnd - lo, (h * G + g) * HEAD_DIM:(h * G + g + 1) * HEAD_DIM] = (
                        o[a - r0:bnd - r0].astype(BF16))

    scores(0)
    for h in range(n_heads):
        if h + 1 < n_heads:
            scores(h + 1)
        weights(h)
        values(h)


def _full_attn(q, kc, vc, k, v, *, G, tq):
    B, T, qw = q.shape
    Tc = kc.shape[1]
    Tk = Tc + T
    return pl.pallas_call(
        functools.partial(_full_attn_kernel, G=G, row_block=32, key_chunk=512),
        grid=(B, T // tq),
        in_specs=[
            pl.BlockSpec((1, tq, qw), lambda b, t: (b, t, 0)),
            pl.BlockSpec((1, Tc, KV_WIDTH), lambda b, t: (b, 0, 0), pipeline_mode=pl.Buffered(1)),
            pl.BlockSpec((1, Tc, V_WIDTH), lambda b, t: (b, 0, 0), pipeline_mode=pl.Buffered(1)),
            pl.BlockSpec((1, T, KV_WIDTH), lambda b, t: (b, 0, 0), pipeline_mode=pl.Buffered(1)),
            pl.BlockSpec((1, T, V_WIDTH), lambda b, t: (b, 0, 0), pipeline_mode=pl.Buffered(1)),
        ],
        out_specs=pl.BlockSpec((1, tq, qw), lambda b, t: (b, t, 0)),
        out_shape=jax.ShapeDtypeStruct((B, T, qw), BF16),
        scratch_shapes=[pltpu.VMEM((2, G * tq, Tk), F32), pltpu.VMEM((2, G * tq, Tk), BF16),
                        pltpu.VMEM((2, G * tq, HEAD_DIM), F32)],
        compiler_params=_cparams("parallel", "arbitrary"),
        name="full_attn",
    )(q, kc, vc, k, v)


def _split3(a):
    hi = a.astype(BF16)
    r1 = a - hi.astype(F32)
    mid = r1.astype(BF16)
    lo = (r1 - mid.astype(F32)).astype(BF16)
    return hi, mid, lo


def _outproj_kernel(*refs, n_y):
    y_refs = refs[:n_y]
    (w_ref, x_ref, g1_ref, ng_ref, sh_ref, sc_ref, rw_ref, xo_ref, h_ref, lg_ref) = refs[n_y:]
    acc = None
    row = 0
    for y_ref in y_refs:
        wd = y_ref.shape[2]
        part = jnp.dot(y_ref[0], w_ref[row:row + wd, :], preferred_element_type=F32)
        acc = part if acc is None else acc + part
        row += wd
    x = x_ref[0] + g1_ref[0] * acc
    xo_ref[0] = x
    h = _norm_mod(x, ng_ref[...], sh_ref[0], sc_ref[0])
    h_ref[0] = h.astype(BF16)
    h_hi, h_mid, h_lo = _split3(h)
    w_hi, w_mid, w_lo = rw_ref[0], rw_ref[1], rw_ref[2]
    dot = functools.partial(jnp.dot, preferred_element_type=F32)
    lg_ref[0] = (dot(h_hi, w_hi) + (dot(h_hi, w_mid) + dot(h_mid, w_hi))
                 + (dot(h_hi, w_lo) + dot(h_mid, w_mid) + dot(h_lo, w_hi)))


def _outproj(ys, w, x, g1, ng, shift, scale, rw3, *, tm):
    B, T, D = x.shape
    per_batch = g1.shape[0] == B
    mod_map = (lambda b, t: (b, 0, 0)) if per_batch else (lambda b, t: (0, 0, 0))
    tile = lambda wd: pl.BlockSpec((1, tm, wd), lambda b, t: (b, t, 0))
    in_specs = [tile(y.shape[2]) for y in ys] + [
        pl.BlockSpec(w.shape, lambda b, t: (0, 0)),
        tile(D),
        pl.BlockSpec((1, 1, D), mod_map),
        pl.BlockSpec((1, D), lambda b, t: (0, 0)),
        pl.BlockSpec((1, 1, D), mod_map),
        pl.BlockSpec((1, 1, D), mod_map),
        pl.BlockSpec(rw3.shape, lambda b, t: (0, 0, 0)),
    ]
    return pl.pallas_call(
        functools.partial(_outproj_kernel, n_y=len(ys)),
        grid=(B, T // tm),
        in_specs=in_specs,
        out_specs=[tile(D), tile(D), tile(N_EXPERTS)],
        out_shape=[jax.ShapeDtypeStruct((B, T, D), F32),
                   jax.ShapeDtypeStruct((B, T, D), BF16),
                   jax.ShapeDtypeStruct((B, T, N_EXPERTS), F32)],
        compiler_params=_cparams("parallel", "parallel"),
        name="outproj",
    )(*ys, w, x, g1, ng.reshape(1, D), shift, scale, rw3)


def _expert_kernel(xs_ref, gate_ref, wg_ref, wu_ref, wd_ref, o_ref, acc_ref):
    f = pl.program_id(2)
    xs = xs_ref[0]
    a = jnp.dot(xs, wg_ref[0], preferred_element_type=F32)
    u = jnp.dot(xs, wu_ref[0], preferred_element_type=F32)
    act = (a * (1.0 / (1.0 + jnp.exp(-a))) * u).astype(BF16)
    part = jnp.dot(act, wd_ref[0], preferred_element_type=F32)

    @pl.when(f == 0)
    def _():
        acc_ref[...] = part

    @pl.when(f > 0)
    def _():
        acc_ref[...] += part

    @pl.when(f == pl.num_programs(2) - 1)
    def _():
        o_ref[0] = acc_ref[...] * gate_ref[0]


def _experts(xs, gate, wg, wu, wd, *, tf=512):
    E, M, D = xs.shape
    F = wg.shape[2]
    tm = M // 4
    return pl.pallas_call(
        _expert_kernel,
        grid=(E, M // tm, F // tf),
        in_specs=[
            pl.BlockSpec((1, tm, D), lambda e, m, f: (e, m, 0)),
            pl.BlockSpec((1, tm, 1), lambda e, m, f: (e, m, 0)),
            pl.BlockSpec((1, D, tf), lambda e, m, f: (e, 0, f)),
            pl.BlockSpec((1, D, tf), lambda e, m, f: (e, 0, f)),
            pl.BlockSpec((1, tf, D), lambda e, m, f: (e, f, 0)),
        ],
        out_specs=pl.BlockSpec((1, tm, D), lambda e, m, f: (e, m, 0)),
        out_shape=jax.ShapeDtypeStruct((E, M, D), F32),
        scratch_shapes=[pltpu.VMEM((tm, D), F32)],
        compiler_params=_cparams("parallel", "parallel", "arbitrary"),
        name="experts",
    )(xs, gate, wg, wu, wd)


def _final_norm_kernel(x_ref, g_ref, o_ref):
    x = x_ref[0]
    o_ref[0] = x * lax.rsqrt(jnp.mean(x * x, axis=-1, keepdims=True) + EPS) * g_ref[...]


def _final_norm(x, g, *, tm=512):
    B, T, D = x.shape
    return pl.pallas_call(
        _final_norm_kernel,
        grid=(B, T // tm),
        in_specs=[pl.BlockSpec((1, tm, D), lambda b, t: (b, t, 0)),
                  pl.BlockSpec((1, D), lambda b, t: (0, 0))],
        out_specs=pl.BlockSpec((1, tm, D), lambda b, t: (b, t, 0)),
        out_shape=jax.ShapeDtypeStruct((B, T, D), F32),
        compiler_params=_cparams("parallel", "parallel"),
        name="final_norm",
    )(x, g.reshape(1, D))


def _rope_tables(n_tokens):
    rows = n_tokens // GRID_W
    row = jnp.broadcast_to(jnp.arange(rows)[:, None], (rows, GRID_W)).reshape(-1).astype(F32)
    col = jnp.broadcast_to(jnp.arange(GRID_W)[None, :], (rows, GRID_W)).reshape(-1).astype(F32)
    inv = ROPE_THETA ** (-jnp.arange(0, ROT_AXIS, 2, dtype=F32) / ROT_AXIS)
    ang_r, ang_c = row[:, None] * inv, col[:, None] * inv
    cos = jnp.concatenate([jnp.cos(ang_r)] * 2 + [jnp.cos(ang_c)] * 2, axis=1)
    sin = jnp.concatenate([-jnp.sin(ang_r), jnp.sin(ang_r), -jnp.sin(ang_c), jnp.sin(ang_c)], axis=1)
    return cos, sin


def _route(h, logits):
    B, T, D = h.shape
    cap = CAPACITY_FACTOR * T // N_EXPERTS
    aff = jax.nn.softmax(logits, axis=-1)
    gate, idx = lax.top_k(jnp.swapaxes(aff, 1, 2), cap)
    flat = idx + (jnp.arange(B, dtype=idx.dtype) * T)[:, None, None]
    flat = jnp.swapaxes(flat, 0, 1).reshape(N_EXPERTS, B * cap)
    xs = h.reshape(B * T, D)[flat.reshape(-1)].reshape(N_EXPERTS, B * cap, D)
    gate = jnp.swapaxes(gate, 0, 1).reshape(N_EXPERTS, B * cap, 1)
    return xs, gate, flat


def _unroute(ye, flat, B, T):
    E, M, D = ye.shape
    return jnp.zeros((B * T, D), ye.dtype).at[flat.reshape(-1)].add(ye.reshape(E * M, D)).reshape(B, T, D)


def kernel(x, c, ctx, c_ctx, mod_w, mod_b, norm_g, w_out, router_w, expert_w_gate, expert_w_up,
           expert_w_down, ab_w_in, pool_w, pool_scale, ab_sink, c_w_in, c_q_norm_g, c_k_norm_g,
           final_norm_g):
    B, T, D = x.shape
    Tc = ctx.shape[1]
    tables = _rope_tables(T)

    cond = jnp.zeros((COND_ROWS, D), F32).at[:B].set(c).at[B].set(c_ctx)
    mod = _adaln(cond, mod_w, mod_b)

    for i in range(DEPTH):
        last = i == DEPTH - 1
        j = i // 2
        mx = [mod[i, :B, k * D:(k + 1) * D].reshape(B, 1, D) for k in range(6)]
        mc = [mod[i, B:B + 1, k * D:(k + 1) * D].reshape(1, 1, D) for k in range(6)]
        rw3 = jnp.stack(_split3(router_w[i]))
        wo = w_out[i].astype(BF16)
        wg, wu, wd = (expert_w_gate[i].astype(BF16), expert_w_up[i].astype(BF16),
                      expert_w_down[i].astype(BF16))
        if i % 2 == 0:
            w_in = ab_w_in[j].astype(BF16)
            u, q, k, v = _inproj(x, norm_g[i, 0], mx[0], mx[1], w_in, n_pool=4, n_q=B_HEADS,
                                 tm=512, tables=tables)
            uc, qc, kc, vc = _inproj(ctx, norm_g[i, 0], mc[0], mc[1], w_in, n_pool=4, n_q=B_HEADS, tm=Tc)
            pw = pool_w[j].astype(BF16)
            ys = [_pool(u, pw, pool_scale[j]), _win_attn(ab_sink[j], q, k, v, kc, vc)]
            if not last:
                ycs = [_pool(uc, pw, pool_scale[j]),
                       _ctx_attn(qc, kc, vc, G=B_HEADS // B_KV_HEADS, sink=ab_sink[j])]
        else:
            w_in = c_w_in[j].astype(BF16)
            gains = (c_q_norm_g[j], c_k_norm_g[j])
            q, k, v = _inproj(x, norm_g[i, 0], mx[0], mx[1], w_in, n_pool=0, n_q=C_HEADS,
                              tm=512, tables=tables, qk_gains=gains)
            qc, kc, vc = _inproj(ctx, norm_g[i, 0], mc[0], mc[1], w_in, n_pool=0, n_q=C_HEADS,
                                 tm=Tc, qk_gains=gains)
            ys = [_full_attn(q, kc, vc, k, v, G=C_HEADS // C_KV_HEADS, tq=128)]
            if not last:
                ycs = [_ctx_attn(qc, kc, vc, G=C_HEADS // C_KV_HEADS)]

        x, h2, lg = _outproj(ys, wo, x, mx[2], norm_g[i, 1], mx[3], mx[4], rw3, tm=512)
        xs, gate, idx = _route(h2, lg)
        if not last:
            ctx, h2c, lgc = _outproj(ycs, wo, ctx, mc[2], norm_g[i, 1], mc[3], mc[4], rw3, tm=Tc)
            xsc, gatec, idxc = _route(h2c, lgc)
            n_lat = xs.shape[1]
            ye = _experts(jnp.concatenate([xs, xsc], axis=1), jnp.concatenate([gate, gatec], axis=1),
                          wg, wu, wd)
            ctx = ctx + mc[5] * _unroute(ye[:, n_lat:], idxc, B, Tc)
            ye = ye[:, :n_lat]
        else:
            ye = _experts(xs, gate, wg, wu, wd)
        x = x + mx[5] * _unroute(ye, idx, B, T)

    return _final_norm(x, final_norm_g)
```

```python
import functools

import jax
import jax.numpy as jnp
from jax import lax
from jax.experimental import pallas as pl
from jax.experimental.pallas import tpu as pltpu

F32 = jnp.float32
BF16 = jnp.bfloat16
I32 = jnp.int32
LANES = 128

D_MODEL = 2048
DEPTH = 2
GRID_W = 64
HEAD_DIM = 128
EPS = 1e-6
NEG_INF = -1e30
ATTN_SCALE = HEAD_DIM ** -0.5
POOL_WINDOWS = (2, 4, 8, 16)
POOL_GROUP = 128
POOL_WIDTH = 512
WINDOW = 128
B_HEADS = 12
B_KV_HEADS = 4
C_HEADS = 16
C_KV_HEADS = 4
KV_WIDTH = 4 * HEAD_DIM
V_HEAD = 2 * HEAD_DIM
V_WIDTH = 4 * V_HEAD
LOG2E = 1.4426950408889634
Q_SCALE = ATTN_SCALE * LOG2E
ROT_AXIS = HEAD_DIM // 2
ROPE_THETA = 10000.0
N_EXPERTS = 16
CAPACITY_FACTOR = 2
COND_ROWS = 16
VMEM_LIMIT = 56 * 1024 * 1024


def _cparams(*sem):
    return pltpu.CompilerParams(dimension_semantics=sem, vmem_limit_bytes=VMEM_LIMIT)


def _adaln_kernel(cond_ref, w_ref, b_ref, o_ref):
    cnd = cond_ref[...]
    s = cnd * (1.0 / (1.0 + jnp.exp(-cnd)))
    o_ref[0] = jnp.dot(s.astype(BF16), w_ref[0].astype(BF16), preferred_element_type=F32) + b_ref[0]


def _adaln(cond, mod_w, mod_b):
    tn = 1024
    six_d = mod_w.shape[-1]
    return pl.pallas_call(
        _adaln_kernel,
        grid=(DEPTH, six_d // tn),
        in_specs=[
            pl.BlockSpec((COND_ROWS, D_MODEL), lambda i, j: (0, 0)),
            pl.BlockSpec((1, D_MODEL, tn), lambda i, j: (i, 0, j)),
            pl.BlockSpec((1, 1, tn), lambda i, j: (i, 0, j)),
        ],
        out_specs=pl.BlockSpec((1, COND_ROWS, tn), lambda i, j: (i, 0, j)),
        out_shape=jax.ShapeDtypeStruct((DEPTH, COND_ROWS, six_d), F32),
        compiler_params=_cparams("parallel", "parallel"),
        name="adaln",
    )(cond, mod_w, mod_b.reshape(DEPTH, 1, six_d))


def _norm_mod(x, g, shift, scale):
    ms = jnp.mean(x * x, axis=-1, keepdims=True)
    h = x * lax.rsqrt(ms + EPS) * g
    return h * (1.0 + scale) + shift


def _rope(a, cos, sin_signed):
    lane = lax.broadcasted_iota(jnp.int32, a.shape, 1)
    first = (lane & (ROT_AXIS - 1)) < (ROT_AXIS // 2)
    partner = jnp.where(first, pltpu.roll(a, HEAD_DIM - ROT_AXIS // 2, axis=1),
                        pltpu.roll(a, ROT_AXIS // 2, axis=1))
    return a * cos + partner * sin_signed


def _inproj_kernel(*refs, n_pool, n_q, qk_norm, rope):
    x_ref, g_ref, sh_ref, sc_ref, w_ref = refs[:5]
    pos = 5
    if rope:
        cos_ref, sin_ref = refs[pos:pos + 2]
        pos += 2
    if qk_norm:
        qg_ref, kg_ref = refs[pos:pos + 2]
        pos += 2
    outs = refs[pos:]
    if n_pool:
        u_ref, q_ref, k_ref, v_ref = outs
    else:
        q_ref, k_ref, v_ref = outs

    hb = _norm_mod(x_ref[0], g_ref[...], sh_ref[0], sc_ref[0]).astype(BF16)
    n_heads = w_ref.shape[1] // HEAD_DIM
    group = 4
    for c0 in range(0, n_heads, group):
        acc = jnp.dot(hb, w_ref[:, c0 * HEAD_DIM:(c0 + group) * HEAD_DIM], preferred_element_type=F32)
        for j in range(group):
            c = c0 + j
            a = acc[:, j * HEAD_DIM:(j + 1) * HEAD_DIM]
            if c < n_pool:
                u_ref[0, :, c * HEAD_DIM:(c + 1) * HEAD_DIM] = a
                continue
            c -= n_pool
            is_q = c < n_q
            is_k = (not is_q) and c < n_q + 4
            if is_q or is_k:
                if qk_norm:
                    gain = qg_ref[...] if is_q else kg_ref[...]
                    a = a * lax.rsqrt(jnp.mean(a * a, axis=-1, keepdims=True) + EPS) * gain
                if rope:
                    a = _rope(a, cos_ref[...], sin_ref[...])
            if is_q:
                q_ref[0, :, c * HEAD_DIM:(c + 1) * HEAD_DIM] = (a * Q_SCALE).astype(BF16)
            elif is_k:
                c -= n_q
                k_ref[0, :, c * HEAD_DIM:(c + 1) * HEAD_DIM] = a.astype(BF16)
            else:
                c -= n_q + 4
                v_ref[0, :, c * V_HEAD:c * V_HEAD + HEAD_DIM] = a.astype(BF16)
                v_ref[0, :, c * V_HEAD + HEAD_DIM:(c + 1) * V_HEAD] = jnp.ones(a.shape, BF16)


def _inproj(x, g, shift, scale, w, *, n_pool, n_q, tm, tables=None, qk_gains=None):
    B, T, D = x.shape
    per_batch = shift.shape[0] == B
    mod_map = (lambda b, t: (b, 0, 0)) if per_batch else (lambda b, t: (0, 0, 0))
    in_specs = [
        pl.BlockSpec((1, tm, D), lambda b, t: (b, t, 0)),
        pl.BlockSpec((1, D), lambda b, t: (0, 0)),
        pl.BlockSpec((1, 1, D), mod_map),
        pl.BlockSpec((1, 1, D), mod_map),
        pl.BlockSpec(w.shape, lambda b, t: (0, 0)),
    ]
    args = [x, g.reshape(1, D), shift, scale, w]
    if tables is not None:
        in_specs += [pl.BlockSpec((tm, HEAD_DIM), lambda b, t: (t, 0))] * 2
        args += list(tables)
    if qk_gains is not None:
        in_specs += [pl.BlockSpec((1, HEAD_DIM), lambda b, t: (0, 0))] * 2
        args += [qk_gains[0].reshape(1, HEAD_DIM), qk_gains[1].reshape(1, HEAD_DIM)]
    widths = ([(n_pool * HEAD_DIM, F32)] if n_pool else []) + [
        (n_q * HEAD_DIM, BF16), (KV_WIDTH, BF16), (V_WIDTH, BF16)]
    out_specs = [pl.BlockSpec((1, tm, wd), lambda b, t: (b, t, 0)) for wd, _ in widths]
    out_shape = [jax.ShapeDtypeStruct((B, T, wd), dt) for wd, dt in widths]
    return pl.pallas_call(
        functools.partial(_inproj_kernel, n_pool=n_pool, n_q=n_q,
                          qk_norm=qk_gains is not None, rope=tables is not None),
        grid=(B, T // tm),
        in_specs=in_specs, out_specs=out_specs, out_shape=out_shape,
        compiler_params=_cparams("parallel", "parallel"),
        name="inproj",
    )(*args)


def _pool_kernel(u_ref, w_ref, s_ref, o_ref, pad_ref, *, T, chunk):
    halo = 16
    pad_ref[0:halo, :] = jnp.zeros((halo, POOL_WIDTH), F32)
    pad_ref[halo + T:2 * halo + T, :] = jnp.zeros((halo, POOL_WIDTH), F32)
    pad_ref[halo:halo + T, :] = u_ref[0]
    for g, wdw in enumerate(POOL_WINDOWS):
        cols = slice(g * POOL_GROUP, (g + 1) * POOL_GROUP)
        half = wdw // 2
        for r0 in range(0, T, chunk):
            s = pad_ref[halo + r0 - half:halo + r0 - half + chunk, cols]
            for j in range(-half + 1, half):
                s = s + pad_ref[halo + r0 + j:halo + r0 + j + chunk, cols]
            t = r0 + lax.broadcasted_iota(jnp.int32, (chunk, 1), 0)
            cnt = (jnp.minimum(t + half, T) - jnp.maximum(t - half, 0)).astype(F32)
            d = (s / cnt - u_ref[0, r0:r0 + chunk, cols]).astype(BF16)
            y = jnp.dot(d, w_ref[g], preferred_element_type=F32) * s_ref[:, cols]
            o_ref[0, r0:r0 + chunk, cols] = y.astype(BF16)


def _pool(u, pool_w, pool_scale):
    B, T, _ = u.shape
    return pl.pallas_call(
        functools.partial(_pool_kernel, T=T, chunk=min(T, 512)),
        grid=(B,),
        in_specs=[
            pl.BlockSpec((1, T, POOL_WIDTH), lambda b: (b, 0, 0)),
            pl.BlockSpec(pool_w.shape, lambda b: (0, 0, 0)),
            pl.BlockSpec((1, POOL_WIDTH), lambda b: (0, 0)),
        ],
        out_specs=pl.BlockSpec((1, T, POOL_WIDTH), lambda b: (b, 0, 0)),
        out_shape=jax.ShapeDtypeStruct((B, T, POOL_WIDTH), BF16),
        scratch_shapes=[pltpu.VMEM((T + 32, POOL_WIDTH), F32)],
        compiler_params=_cparams("parallel"),
        name="pool",
    )(u, pool_w, pool_scale.reshape(1, POOL_WIDTH))


def _stack_heads(q_ref, h, G):
    return jnp.concatenate(
        [q_ref[0, :, (h * G + g) * HEAD_DIM:(h * G + g + 1) * HEAD_DIM] for g in range(G)], axis=0)


def _qk(q, k):
    return lax.dot_general(q, k, (((1,), (1,)), ((), ())), preferred_element_type=F32)


def _pv(p, v):
    return jnp.dot(p.astype(BF16), v, preferred_element_type=F32)


def _sink_column(sink_ref, h, G, tq):
    return jnp.concatenate([jnp.full((tq, 1), sink_ref[h * G + g] * LOG2E, F32) for g in range(G)], axis=0)


def _store_heads(o_ref, o, h, G, tq):
    for g in range(G):
        o_ref[0, :, (h * G + g) * HEAD_DIM:(h * G + g + 1) * HEAD_DIM] = o[g * tq:(g + 1) * tq].astype(BF16)


def _win_attn_kernel(sink_ref, q_ref, k_ref, v_ref, kc_ref, vc_ref, o_ref, *, T, tq):
    G = B_HEADS // B_KV_HEADS
    kw_len = tq + 2 * WINDOW
    q0 = pl.program_id(1) * tq
    start = pl.multiple_of(jnp.clip(q0 - WINDOW, 0, T - kw_len), WINDOW)
    delta = q0 - start
    r = lax.broadcasted_iota(jnp.int32, (G * tq, kw_len), 0) & (tq - 1)
    c = lax.broadcasted_iota(jnp.int32, (G * tq, kw_len), 1)
    band = jnp.abs(r + delta - c) <= WINDOW
    for h in range(B_KV_HEADS):
        hs = slice(h * HEAD_DIM, (h + 1) * HEAD_DIM)
        vs = slice(h * V_HEAD, (h + 1) * V_HEAD)
        qs = _stack_heads(q_ref, h, G)
        s_loc = jnp.where(band, _qk(qs, k_ref[0, pl.ds(start, kw_len), hs]), NEG_INF)
        s_ctx = _qk(qs, kc_ref[0, :, hs])
        sink = _sink_column(sink_ref, h, G, tq)
        m = jnp.maximum(jnp.maximum(jnp.max(s_loc, axis=-1, keepdims=True),
                                    jnp.max(s_ctx, axis=-1, keepdims=True)), sink)
        oa = (_pv(jnp.exp2(s_loc - m), v_ref[0, pl.ds(start, kw_len), vs])
              + _pv(jnp.exp2(s_ctx - m), vc_ref[0, :, vs]))
        o = oa[:, :HEAD_DIM] / (oa[:, HEAD_DIM:] + jnp.exp2(sink - m))
        _store_heads(o_ref, o, h, G, tq)


def _win_attn(sink, q, k, v, kc, vc, *, tq=128):
    B, T, qw = q.shape
    Tc = kc.shape[1]
    return pl.pallas_call(
        functools.partial(_win_attn_kernel, T=T, tq=tq),
        grid=(B, T // tq),
        in_specs=[
            pl.BlockSpec(memory_space=pltpu.SMEM),
            pl.BlockSpec((1, tq, qw), lambda b, t: (b, t, 0)),
            pl.BlockSpec((1, T, KV_WIDTH), lambda b, t: (b, 0, 0)),
            pl.BlockSpec((1, T, V_WIDTH), lambda b, t: (b, 0, 0)),
            pl.BlockSpec((1, Tc, KV_WIDTH), lambda b, t: (b, 0, 0)),
            pl.BlockSpec((1, Tc, V_WIDTH), lambda b, t: (b, 0, 0)),
        ],
        out_specs=pl.BlockSpec((1, tq, qw), lambda b, t: (b, t, 0)),
        out_shape=jax.ShapeDtypeStruct((B, T, qw), BF16),
        compiler_params=_cparams("parallel", "arbitrary"),
        name="win_attn",
    )(sink, q, k, v, kc, vc)


def _ctx_attn_kernel(*refs, G, has_sink):
    if has_sink:
        sink_ref, q_ref, k_ref, v_ref, o_ref = refs
    else:
        q_ref, k_ref, v_ref, o_ref = refs
    tq = q_ref.shape[1]
    for h in range(q_ref.shape[2] // (G * HEAD_DIM)):
        qs = _stack_heads(q_ref, h, G)
        s = _qk(qs, k_ref[0, :, h * HEAD_DIM:(h + 1) * HEAD_DIM])
        m = jnp.max(s, axis=-1, keepdims=True)
        if has_sink:
            sink = _sink_column(sink_ref, h, G, tq)
            m = jnp.maximum(m, sink)
        oa = _pv(jnp.exp2(s - m), v_ref[0, :, h * V_HEAD:(h + 1) * V_HEAD])
        den = oa[:, HEAD_DIM:]
        if has_sink:
            den = den + jnp.exp2(sink - m)
        _store_heads(o_ref, oa[:, :HEAD_DIM] / den, h, G, tq)


def _ctx_attn(q, k, v, *, G, sink=None):
    B, Tc, qw = q.shape
    in_specs, args = [], []
    if sink is not None:
        in_specs.append(pl.BlockSpec(memory_space=pltpu.SMEM))
        args.append(sink)
    in_specs += [pl.BlockSpec((1, Tc, a.shape[2]), lambda b: (b, 0, 0)) for a in (q, k, v)]
    return pl.pallas_call(
        functools.partial(_ctx_attn_kernel, G=G, has_sink=sink is not None),
        grid=(B,),
        in_specs=in_specs,
        out_specs=pl.BlockSpec((1, Tc, qw), lambda b: (b, 0, 0)),
        out_shape=jax.ShapeDtypeStruct((B, Tc, qw), BF16),
        compiler_params=_cparams("parallel"),
        name="ctx_attn",
    )(*args, q, k, v)


def _lane_tile_max(s):
    parts = [s[:, t * HEAD_DIM:(t + 1) * HEAD_DIM] for t in range(s.shape[1] // HEAD_DIM)]
    while len(parts) > 1:
        parts = [jnp.maximum(parts[i], parts[i + 1]) if i + 1 < len(parts) else parts[i]
                 for i in range(0, len(parts), 2)]
    return parts[0]


def _full_attn_kernel(q_ref, kc_ref, vc_ref, k_ref, v_ref, o_ref, s_ref, p_ref, m_ref, *, G, row_block, key_chunk):
    tq = q_ref.shape[1]
    R = G * tq
    half = R // 2
    Tc, Tk = kc_ref.shape[1], s_ref.shape[2]
    n_lane_tiles = Tk // HEAD_DIM
    n_heads = q_ref.shape[2] // (G * HEAD_DIM)

    def scores(h):
        hs = slice(h * HEAD_DIM, (h + 1) * HEAD_DIM)
        qs = _stack_heads(q_ref, h, G)
        for r0 in (0, half):
            q_half = qs[r0:r0 + half]
            s = _qk(q_half, kc_ref[0, :, hs])
            s_ref[h % 2, r0:r0 + half, 0:Tc] = s
            m_run = _lane_tile_max(s)
            for c0 in range(0, Tk - Tc, key_chunk):
                s = _qk(q_half, k_ref[0, c0:c0 + key_chunk, hs])
                s_ref[h % 2, r0:r0 + half, Tc + c0:Tc + c0 + key_chunk] = s
                m_run = jnp.maximum(m_run, _lane_tile_max(s))
            m_ref[h % 2, r0:r0 + half, :] = jnp.broadcast_to(
                jnp.max(m_run, axis=-1, keepdims=True), (half, HEAD_DIM))

    def weights(h):
        for r0 in range(0, R, row_block):
            mb = m_ref[h % 2, r0:r0 + row_block, :]
            for t in range(n_lane_tiles):
                cols = slice(t * HEAD_DIM, (t + 1) * HEAD_DIM)
                p_ref[h % 2, r0:r0 + row_block, cols] = jnp.exp2(
                    s_ref[h % 2, r0:r0 + row_block, cols] - mb).astype(BF16)

    def values(h):
        vs = slice(h * V_HEAD, (h + 1) * V_HEAD)
        for r0 in (0, half):
            oa = (jnp.dot(p_ref[h % 2, r0:r0 + half, 0:Tc], vc_ref[0, :, vs], preferred_element_type=F32)
                  + jnp.dot(p_ref[h % 2, r0:r0 + half, Tc:Tk], v_ref[0, :, vs], preferred_element_type=F32))
            o = oa[:, :HEAD_DIM] / oa[:, HEAD_DIM:]
            for g in range(G):
                lo, hi = g * tq, (g + 1) * tq
                a, bnd = max(lo, r0), min(hi, r0 + half)
                if a < bnd:
                    o_ref[0, a - lo:bnd - lo, (h * G + g) * HEAD_DIM:(h * G + g + 1) * HEAD_DIM] = (
                        o[a - r0:bnd - r0].astype(BF16))

    scores(0)
    for h in range(n_heads):
        if h + 1 < n_heads:
            scores(h + 1)
        weights(h)
        values(h)


def _full_attn(q, kc, vc, k, v, *, G, tq):
    B, T, qw = q.shape
    Tc = kc.shape[1]
    Tk = Tc + T
    return pl.pallas_call(
        functools.partial(_full_attn_kernel, G=G, row_block=32, key_chunk=512),
        grid=(B, T // tq),
        in_specs=[
            pl.BlockSpec((1, tq, qw), lambda b, t: (b, t, 0)),
            pl.BlockSpec((1, Tc, KV_WIDTH), lambda b, t: (b, 0, 0), pipeline_mode=pl.Buffered(1)),
            pl.BlockSpec((1, Tc, V_WIDTH), lambda b, t: (b, 0, 0), pipeline_mode=pl.Buffered(1)),
            pl.BlockSpec((1, T, KV_WIDTH), lambda b, t: (b, 0, 0), pipeline_mode=pl.Buffered(1)),
            pl.BlockSpec((1, T, V_WIDTH), lambda b, t: (b, 0, 0), pipeline_mode=pl.Buffered(1)),
        ],
        out_specs=pl.BlockSpec((1, tq, qw), lambda b, t: (b, t, 0)),
        out_shape=jax.ShapeDtypeStruct((B, T, qw), BF16),
        scratch_shapes=[pltpu.VMEM((2, G * tq, Tk), F32), pltpu.VMEM((2, G * tq, Tk), BF16),
                        pltpu.VMEM((2, G * tq, HEAD_DIM), F32)],
        compiler_params=_cparams("parallel", "arbitrary"),
        name="full_attn",
    )(q, kc, vc, k, v)


def _split3(a):
    hi = a.astype(BF16)
    r1 = a - hi.astype(F32)
    mid = r1.astype(BF16)
    lo = (r1 - mid.astype(F32)).astype(BF16)
    return hi, mid, lo


def _outproj_kernel(*refs, n_y):
    y_refs = refs[:n_y]
    (w_ref, x_ref, g1_ref, ng_ref, sh_ref, sc_ref, rw_ref, xo_ref, h_ref, lg_ref) = refs[n_y:]
    acc = None
    row = 0
    for y_ref in y_refs:
        wd = y_ref.shape[2]
        part = jnp.dot(y_ref[0], w_ref[row:row + wd, :], preferred_element_type=F32)
        acc = part if acc is None else acc + part
        row += wd
    x = x_ref[0] + g1_ref[0] * acc
    xo_ref[0] = x
    h = _norm_mod(x, ng_ref[...], sh_ref[0], sc_ref[0])
    h_ref[0] = h
    h_hi, h_mid, h_lo = _split3(h)
    E = N_EXPERTS
    dot = functools.partial(jnp.dot, preferred_element_type=F32)
    p_hi = dot(h_hi, rw_ref[...])
    p_mid = dot(h_mid, rw_ref[:, :2 * E])
    p_lo = dot(h_lo, rw_ref[:, :E])
    lg_ref[0] = (p_hi[:, :E] + (p_hi[:, E:2 * E] + p_mid[:, :E])
                 + (p_hi[:, 2 * E:] + p_mid[:, E:] + p_lo))


def _outproj(ys, w, x, g1, ng, shift, scale, rw3, *, tm):
    B, T, D = x.shape
    per_batch = g1.shape[0] == B
    mod_map = (lambda b, t: (b, 0, 0)) if per_batch else (lambda b, t: (0, 0, 0))
    tile = lambda wd: pl.BlockSpec((1, tm, wd), lambda b, t: (b, t, 0))
    in_specs = [tile(y.shape[2]) for y in ys] + [
        pl.BlockSpec(w.shape, lambda b, t: (0, 0)),
        tile(D),
        pl.BlockSpec((1, 1, D), mod_map),
        pl.BlockSpec((1, D), lambda b, t: (0, 0)),
        pl.BlockSpec((1, 1, D), mod_map),
        pl.BlockSpec((1, 1, D), mod_map),
        pl.BlockSpec(rw3.shape, lambda b, t: (0, 0)),
    ]
    return pl.pallas_call(
        functools.partial(_outproj_kernel, n_y=len(ys)),
        grid=(B, T // tm),
        in_specs=in_specs,
        out_specs=[tile(D), tile(D), tile(N_EXPERTS)],
        out_shape=[jax.ShapeDtypeStruct((B, T, D), F32),
                   jax.ShapeDtypeStruct((B, T, D), F32),
                   jax.ShapeDtypeStruct((B, T, N_EXPERTS), F32)],
        compiler_params=_cparams("parallel", "parallel"),
        name="outproj",
    )(*ys, w, x, g1, ng.reshape(1, D), shift, scale, rw3)


def _token_prefix(x, tri):
    T = x.shape[0]
    xb = x.astype(BF16)
    local = [jnp.dot(tri, xb[j:j + LANES], preferred_element_type=F32) for j in range(0, T, LANES)]
    out, run = [], None
    for blk in local:
        blk = blk if run is None else blk + run
        run = blk[LANES - 1:LANES, :]
        out.append(blk)
    return jnp.concatenate(out, axis=0)


def _route_kernel(lg_ref, idx_ref, gate_ref, sel_ref, slot_ref, lo_ref, aff_ref, pos_ref, *, T, cap, tile):
    E = N_EXPERTS
    lg = lg_ref[0]
    ex = jnp.exp(lg - jnp.max(lg, axis=-1, keepdims=True))
    aff = ex / jnp.sum(ex, axis=-1, keepdims=True)
    bits = pltpu.bitcast(aff, I32)

    def bisect(_, c):
        lo, hi = c
        mid = lo + ((hi - lo) >> 1)
        cnt = jnp.sum(jnp.where(bits >= mid, 1.0, 0.0), axis=0, keepdims=True)
        ge = cnt >= cap
        return jnp.where(ge, mid, lo), jnp.where(ge, hi, mid)

    thr, _ = lax.fori_loop(0, 31, bisect, (jnp.zeros((1, E), I32), jnp.full((1, E), 0x3F800001, I32)))
    gt = bits > thr
    eq = bits == thr
    r = lax.broadcasted_iota(I32, (LANES, LANES), 0)
    cidx = lax.broadcasted_iota(I32, (LANES, LANES), 1)
    tri = jnp.where(cidx <= r, 1.0, 0.0).astype(BF16)
    eq_f = jnp.where(eq, 1.0, 0.0)
    need = cap - jnp.sum(jnp.where(gt, 1.0, 0.0), axis=0, keepdims=True)
    tie_rank = _token_prefix(eq_f, tri) - eq_f
    sel = gt | (eq & (tie_rank < need))
    sel_f = jnp.where(sel, 1.0, 0.0)
    pos = _token_prefix(sel_f, tri)
    sel_ref[0] = sel_f
    slot_ref[0] = pos - 1.0
    lo_ref[0, 0:1, :] = jnp.zeros((1, E), I32)
    for j in range(1, T // tile + 1):
        lo_ref[0, j:j + 1, :] = pos[j * tile - 1:j * tile, :].astype(I32)
    aff_ref[...] = aff
    pos_ref[...] = jnp.where(sel, pos, 0.0)

    chunk = min(T, 256)
    n_lt = (cap + LANES - 1) // LANES
    for e in range(E):
        def body(ci, accs):
            rows = pl.ds(pl.multiple_of(ci * chunk, chunk), chunk)
            p_col = jnp.broadcast_to(pos_ref[rows, e:e + 1], (chunk, LANES))
            a_col = jnp.broadcast_to(aff_ref[rows, e:e + 1], (chunk, LANES))
            tok = (ci * chunk + lax.broadcasted_iota(I32, (chunk, LANES), 0)).astype(F32)
            out = []
            for lt in range(n_lt):
                want = (lax.broadcasted_iota(I32, (chunk, LANES), 1) + (lt * LANES + 1)).astype(F32)
                hit = p_col == want
                ti = jnp.where(hit, tok, 0.0).reshape(chunk // 8, 8, LANES).sum(axis=0)
                gi = jnp.where(hit, a_col, 0.0).reshape(chunk // 8, 8, LANES).sum(axis=0)
                out += [accs[2 * lt] + ti, accs[2 * lt + 1] + gi]
            return tuple(out)

        accs = lax.fori_loop(0, T // chunk, body, tuple(jnp.zeros((8, LANES), F32) for _ in range(2 * n_lt)))
        for lt in range(n_lt):
            w = min(LANES, cap - lt * LANES)
            idx_ref[0, e:e + 1, lt * LANES:lt * LANES + w] = (
                jnp.sum(accs[2 * lt], axis=0, keepdims=True)[:, :w].astype(I32))
            gate_ref[0, e:e + 1, lt * LANES:lt * LANES + w] = jnp.sum(accs[2 * lt + 1], axis=0, keepdims=True)[:, :w]


def _route(logits, *, cap, tile):
    B, T, E = logits.shape
    nb = T // tile + 1
    tok_spec = pl.BlockSpec((1, T, E), lambda b: (b, 0, 0))
    slot_spec = pl.BlockSpec((1, E, cap), lambda b: (b, 0, 0))
    return pl.pallas_call(
        functools.partial(_route_kernel, T=T, cap=cap, tile=tile),
        grid=(B,),
        in_specs=[tok_spec],
        out_specs=[slot_spec, slot_spec, tok_spec, tok_spec, pl.BlockSpec((1, nb, E), lambda b: (b, 0, 0))],
        out_shape=[jax.ShapeDtypeStruct((B, E, cap), I32), jax.ShapeDtypeStruct((B, E, cap), F32),
                   jax.ShapeDtypeStruct((B, T, E), F32), jax.ShapeDtypeStruct((B, T, E), F32),
                   jax.ShapeDtypeStruct((B, nb, E), I32)],
        scratch_shapes=[pltpu.VMEM((T, E), F32), pltpu.VMEM((T, E), F32)],
        compiler_params=_cparams("arbitrary"),
        name="route",
    )(logits)


GATHER_SPLIT = 4


def _expert_kernel(idx_ref, h_hbm, gate_ref, wg_ref, wu_ref, wd_ref, o_ref, rows_ref, xs_ref, acc_ref, sem,
                   *, tm, n_tiles):
    n, f = pl.program_id(0), pl.program_id(1)
    nf = pl.num_programs(1)
    per = tm // GATHER_SPLIT
    slot = lax.rem(n, 2)

    def row_copy(tile, r, buf):
        return pltpu.make_async_copy(h_hbm.at[pl.ds(idx_ref[tile * tm + r], 1), :],
                                     rows_ref.at[buf, pl.ds(r, 1), :], sem.at[buf])

    def issue(tile, part, buf):
        for i in range(per):
            row_copy(tile, part * per + i, buf).start()

    @pl.when((n == 0) & (f == 0))
    def _():
        for part in range(GATHER_SPLIT):
            issue(0, part, 0)

    @pl.when((n + 1 < n_tiles) & (f < GATHER_SPLIT))
    def _():
        issue(n + 1, f, 1 - slot)

    @pl.when(f == 0)
    def _():
        pltpu.make_async_copy(h_hbm.at[pl.ds(0, tm), :], rows_ref.at[slot], sem.at[slot]).wait()
        xs_ref[...] = rows_ref[slot].astype(BF16)

    xs = xs_ref[...]
    a = jnp.dot(xs, wg_ref[0], preferred_element_type=F32)
    u = jnp.dot(xs, wu_ref[0], preferred_element_type=F32)
    act = (a * (1.0 / (1.0 + jnp.exp(-a))) * u).astype(BF16)
    part = jnp.dot(act, wd_ref[0], preferred_element_type=F32)

    @pl.when(f == 0)
    def _():
        acc_ref[...] = part

    @pl.when(f > 0)
    def _():
        acc_ref[...] += part

    @pl.when(f == nf - 1)
    def _():
        blk = min(tm, LANES)
        eye = lax.broadcasted_iota(I32, (blk, blk), 0) == lax.broadcasted_iota(I32, (blk, blk), 1)
        for r0 in range(0, tm, blk):
            g_row = jnp.broadcast_to(gate_ref[0, :, r0:r0 + blk], (blk, blk))
            g_col = jnp.sum(jnp.where(eye, g_row, 0.0), axis=-1, keepdims=True)
            o_ref[r0:r0 + blk, :] = (acc_ref[r0:r0 + blk, :] * g_col).astype(BF16)


def _experts(idx, h, gate, wg, wu, wd, *, tm, tf=512):
    N, D = h.shape
    E, _, F = wg.shape
    n_tiles = idx.shape[0] // tm
    per_e = n_tiles // E
    assert F // tf >= GATHER_SPLIT and tm % GATHER_SPLIT == 0
    return pl.pallas_call(
        functools.partial(_expert_kernel, tm=tm, n_tiles=n_tiles),
        grid_spec=pltpu.PrefetchScalarGridSpec(
            num_scalar_prefetch=1,
            grid=(n_tiles, F // tf),
            in_specs=[
                pl.BlockSpec(memory_space=pl.ANY),
                pl.BlockSpec((1, 1, tm), lambda n, f, idx: (n, 0, 0)),
                pl.BlockSpec((1, D, tf), lambda n, f, idx: (n // per_e, 0, f)),
                pl.BlockSpec((1, D, tf), lambda n, f, idx: (n // per_e, 0, f)),
                pl.BlockSpec((1, tf, D), lambda n, f, idx: (n // per_e, f, 0)),
            ],
            out_specs=pl.BlockSpec((tm, D), lambda n, f, idx: (n, 0)),
            scratch_shapes=[pltpu.VMEM((2, tm, D), F32), pltpu.VMEM((tm, D), BF16), pltpu.VMEM((tm, D), F32),
                            pltpu.SemaphoreType.DMA((2,))]),
        out_shape=jax.ShapeDtypeStruct((n_tiles * tm, D), BF16),
        compiler_params=_cparams("arbitrary", "arbitrary"),
        name="experts",
    )(idx, h, gate, wg, wu, wd)


def _combine_kernel(lo_ref, x_ref, g_ref, sel_ref, slot_ref, ye_hbm, *rest, cap, win, n_tiles, seg_stride, final):
    if final:
        fg_ref, o_ref, stage_ref, extra_ref, hot_ref, sem = rest
    else:
        o_ref, stage_ref, extra_ref, hot_ref, sem = rest
    E = N_EXPERTS
    b, t = pl.program_id(0), pl.program_id(1)
    step = b * n_tiles + t
    n_steps = pl.num_programs(0) * n_tiles
    buf = lax.rem(step, 2)
    tm = x_ref.shape[1]

    def seg_lo(bb, tt, e):
        return lo_ref[(bb * (n_tiles + 1) + tt) * E + e]

    def win_start(bb, tt, e):
        w0 = jnp.minimum((seg_lo(bb, tt, e) // 16) * 16, cap - win)
        return pl.multiple_of((e * seg_stride + bb) * cap + w0, 16), w0

    def window_copy(bb, tt, e, k):
        row, _ = win_start(bb, tt, e)
        return pltpu.make_async_copy(ye_hbm.at[pl.ds(row, win), :], stage_ref.at[k, pl.ds(e * win, win), :],
                                     sem.at[k])

    @pl.when(step == 0)
    def _():
        for e in range(E):
            window_copy(b, t, e, 0).start()

    @pl.when(step + 1 < n_steps)
    def _():
        nb = jnp.where(t + 1 < n_tiles, b, b + 1)
        nt = jnp.where(t + 1 < n_tiles, t + 1, 0)
        for e in range(E):
            window_copy(nb, nt, e, 1 - buf).start()

    for e in range(E):
        window_copy(b, t, e, buf).wait()

    sel = sel_ref[0] > 0.5
    slot = slot_ref[0]
    lane = lax.broadcasted_iota(I32, (tm, win), 1).astype(F32)
    for e in range(E):
        _, w0 = win_start(b, t, e)
        rel = jnp.where(sel[:, e:e + 1], slot[:, e:e + 1] - w0.astype(F32), -1.0)
        hot_ref[:, e * win:(e + 1) * win] = jnp.where(
            jnp.broadcast_to(rel, (tm, win)) == lane, 1.0, 0.0).astype(BF16)
    D = x_ref.shape[2]
    cw = min(D, 512)
    for c0 in range(0, D, cw):
        cols = slice(c0, c0 + cw)
        moe = jnp.dot(hot_ref[...], stage_ref[buf, :, cols], preferred_element_type=F32)
        o_ref[0, :, cols] = x_ref[0, :, cols] + g_ref[0, :, cols] * moe

    if win < cap:
        for e in range(E):
            _, w0 = win_start(b, t, e)
            hi = seg_lo(b, t + 1, e)
            n_extra = jnp.maximum(hi - (w0 + win) + win - 1, 0) // win

            def extra(k, carry, e=e, w0=w0):
                c0 = w0 + k * win
                ck = jnp.minimum(c0, cap - win)
                row = pl.multiple_of((e * seg_stride + b) * cap + ck, 16)
                cp = pltpu.make_async_copy(ye_hbm.at[pl.ds(row, win), :], extra_ref, sem.at[2])
                cp.start()
                cp.wait()
                s_e = slot[:, e:e + 1]
                ok = sel[:, e:e + 1] & (s_e >= c0.astype(F32)) & (s_e < (c0 + win).astype(F32))
                rel = jnp.where(ok, s_e - ck.astype(F32), -1.0)
                h1 = jnp.where(jnp.broadcast_to(rel, (tm, win)) == lane, 1.0, 0.0).astype(BF16)
                o_ref[0] += g_ref[0] * jnp.dot(h1, extra_ref[...], preferred_element_type=F32)
                return carry

            lax.fori_loop(1, n_extra + 1, extra, 0)

    if final:
        rb = min(tm, 64)

        def norm_rows(i, carry):
            rows = pl.ds(pl.multiple_of(i * rb, rb), rb)
            y = o_ref[0, rows, :]
            o_ref[0, rows, :] = y * lax.rsqrt(jnp.mean(y * y, axis=-1, keepdims=True) + EPS) * fg_ref[...]
            return carry

        lax.fori_loop(0, tm // rb, norm_rows, 0)


def _combine(lo, x, g2, sel, slot, ye, *, cap, tile, seg_stride, final_g=None):
    B, T, D = x.shape
    n_tiles = T // tile
    win = min(cap, LANES)
    per_batch = g2.shape[0] == B
    mod_map = (lambda b, t, lo: (b, 0, 0)) if per_batch else (lambda b, t, lo: (0, 0, 0))
    tok = lambda w: pl.BlockSpec((1, tile, w), lambda b, t, lo: (b, t, 0))
    in_specs = [tok(D), pl.BlockSpec((1, 1, D), mod_map), tok(N_EXPERTS), tok(N_EXPERTS),
                pl.BlockSpec(memory_space=pl.ANY)]
    args = [x, g2, sel, slot, ye]
    if final_g is not None:
        in_specs.append(pl.BlockSpec((1, D), lambda b, t, lo: (0, 0)))
        args.append(final_g.reshape(1, D))
    return pl.pallas_call(
        functools.partial(_combine_kernel, cap=cap, win=win, n_tiles=n_tiles, seg_stride=seg_stride,
                          final=final_g is not None),
        grid_spec=pltpu.PrefetchScalarGridSpec(
            num_scalar_prefetch=1,
            grid=(B, n_tiles),
            in_specs=in_specs,
            out_specs=tok(D),
            scratch_shapes=[pltpu.VMEM((2, N_EXPERTS * win, D), BF16), pltpu.VMEM((win, D), BF16),
                            pltpu.VMEM((tile, N_EXPERTS * win), BF16), pltpu.SemaphoreType.DMA((3,))]),
        out_shape=jax.ShapeDtypeStruct((B, T, D), F32),
        compiler_params=_cparams("arbitrary", "arbitrary"),
        name="combine",
    )(lo.reshape(-1), *args)


def _rope_tables(n_tokens):
    rows = n_tokens // GRID_W
    row = jnp.broadcast_to(jnp.arange(rows)[:, None], (rows, GRID_W)).reshape(-1).astype(F32)
    col = jnp.broadcast_to(jnp.arange(GRID_W)[None, :], (rows, GRID_W)).reshape(-1).astype(F32)
    inv = ROPE_THETA ** (-jnp.arange(0, ROT_AXIS, 2, dtype=F32) / ROT_AXIS)
    ang_r, ang_c = row[:, None] * inv, col[:, None] * inv
    cos = jnp.concatenate([jnp.cos(ang_r)] * 2 + [jnp.cos(ang_c)] * 2, axis=1)
    sin = jnp.concatenate([-jnp.sin(ang_r), jnp.sin(ang_r), -jnp.sin(ang_c), jnp.sin(ang_c)], axis=1)
    return cos, sin


def _moe(h, logits, x, g2, wg, wu, wd, *, tile, final_g=None):
    B, T, D = x.shape
    cap = CAPACITY_FACTOR * T // N_EXPERTS
    idx, gate, sel, slot, lo = _route(logits, cap=cap, tile=tile)
    rows = jnp.swapaxes(idx + (jnp.arange(B, dtype=I32) * T)[:, None, None], 0, 1).reshape(-1)
    tm = min(cap, 512) if cap >= 512 else B * cap
    gate = jnp.swapaxes(gate, 0, 1).reshape(-1, 1, tm)
    ye = _experts(rows, h.reshape(B * T, D), gate, wg, wu, wd, tm=tm)
    return _combine(lo, x, g2, sel, slot, ye, cap=cap, tile=tile, seg_stride=B, final_g=final_g)


def kernel(x, c, ctx, c_ctx, mod_w, mod_b, norm_g, w_out, router_w, expert_w_gate, expert_w_up,
           expert_w_down, ab_w_in, pool_w, pool_scale, ab_sink, c_w_in, c_q_norm_g, c_k_norm_g,
           final_norm_g):
    B, T, D = x.shape
    Tc = ctx.shape[1]
    tables = _rope_tables(T)

    cond = jnp.zeros((COND_ROWS, D), F32).at[:B].set(c).at[B].set(c_ctx)
    mod = _adaln(cond, mod_w, mod_b)

    for i in range(DEPTH):
        last = i == DEPTH - 1
        j = i // 2
        mx = [mod[i, :B, k * D:(k + 1) * D].reshape(B, 1, D) for k in range(6)]
        mc = [mod[i, B:B + 1, k * D:(k + 1) * D].reshape(1, 1, D) for k in range(6)]
        rw3 = jnp.concatenate(_split3(router_w[i]), axis=1)
        wo = w_out[i].astype(BF16)
        wg, wu, wd = (expert_w_gate[i].astype(BF16), expert_w_up[i].astype(BF16),
                      expert_w_down[i].astype(BF16))
        if i % 2 == 0:
            w_in = ab_w_in[j].astype(BF16)
            u, q, k, v = _inproj(x, norm_g[i, 0], mx[0], mx[1], w_in, n_pool=4, n_q=B_HEADS,
                                 tm=512, tables=tables)
            uc, qc, kc, vc = _inproj(ctx, norm_g[i, 0], mc[0], mc[1], w_in, n_pool=4, n_q=B_HEADS, tm=Tc)
            pw = pool_w[j].astype(BF16)
            ys = [_pool(u, pw, pool_scale[j]), _win_attn(ab_sink[j], q, k, v, kc, vc)]
            if not last:
                ycs = [_pool(uc, pw, pool_scale[j]),
                       _ctx_attn(qc, kc, vc, G=B_HEADS // B_KV_HEADS, sink=ab_sink[j])]
        else:
            w_in = c_w_in[j].astype(BF16)
            gains = (c_q_norm_g[j], c_k_norm_g[j])
            q, k, v = _inproj(x, norm_g[i, 0], mx[0], mx[1], w_in, n_pool=0, n_q=C_HEADS,
                              tm=512, tables=tables, qk_gains=gains)
            qc, kc, vc = _inproj(ctx, norm_g[i, 0], mc[0], mc[1], w_in, n_pool=0, n_q=C_HEADS,
                                 tm=Tc, qk_gains=gains)
            ys = [_full_attn(q, kc, vc, k, v, G=C_HEADS // C_KV_HEADS, tq=128)]
            if not last:
                ycs = [_ctx_attn(qc, kc, vc, G=C_HEADS // C_KV_HEADS)]

        x, h2, lg = _outproj(ys, wo, x, mx[2], norm_g[i, 1], mx[3], mx[4], rw3, tm=512)
        x = _moe(h2, lg, x, mx[5], wg, wu, wd, tile=512, final_g=final_norm_g if last else None)
        if not last:
            ctx, h2c, lgc = _outproj(ycs, wo, ctx, mc[2], norm_g[i, 1], mc[3], mc[4], rw3, tm=Tc)
            ctx = _moe(h2c, lgc, ctx, mc[5], wg, wu, wd, tile=Tc)

    return x
```

```python
import functools

import jax
import jax.numpy as jnp
from jax import lax
from jax.experimental import pallas as pl
from jax.experimental.pallas import tpu as pltpu

F32 = jnp.float32
BF16 = jnp.bfloat16
I32 = jnp.int32
U32 = jnp.uint32
LANES = 128

D_MODEL = 2048
DEPTH = 2
GRID_W = 64
HEAD_DIM = 128
EPS = 1e-6
NEG_INF = -1e30
ATTN_SCALE = HEAD_DIM ** -0.5
POOL_WINDOWS = (2, 4, 8, 16)
POOL_GROUP = 128
POOL_WIDTH = 512
WINDOW = 128
B_HEADS = 12
B_KV_HEADS = 4
C_HEADS = 16
C_KV_HEADS = 4
KV_WIDTH = 4 * HEAD_DIM
V_HEAD = 2 * HEAD_DIM
V_WIDTH = 4 * V_HEAD
LOG2E = 1.4426950408889634
Q_SCALE = ATTN_SCALE * LOG2E
ROT_AXIS = HEAD_DIM // 2
ROPE_THETA = 10000.0
N_EXPERTS = 16
CAPACITY_FACTOR = 2
COND_ROWS = 16
VMEM_LIMIT = 56 * 1024 * 1024


def _cparams(*sem):
    return pltpu.CompilerParams(dimension_semantics=sem, vmem_limit_bytes=VMEM_LIMIT)


def _adaln_kernel(cond_ref, w_ref, b_ref, o_ref):
    cnd = cond_ref[...]
    s = cnd * (1.0 / (1.0 + jnp.exp(-cnd)))
    o_ref[0] = jnp.dot(s.astype(BF16), w_ref[0].astype(BF16), preferred_element_type=F32) + b_ref[0]


def _adaln(cond, mod_w, mod_b):
    tn = 1024
    six_d = mod_w.shape[-1]
    return pl.pallas_call(
        _adaln_kernel,
        grid=(DEPTH, six_d // tn),
        in_specs=[
            pl.BlockSpec((COND_ROWS, D_MODEL), lambda i, j: (0, 0)),
            pl.BlockSpec((1, D_MODEL, tn), lambda i, j: (i, 0, j)),
            pl.BlockSpec((1, 1, tn), lambda i, j: (i, 0, j)),
        ],
        out_specs=pl.BlockSpec((1, COND_ROWS, tn), lambda i, j: (i, 0, j)),
        out_shape=jax.ShapeDtypeStruct((DEPTH, COND_ROWS, six_d), F32),
        compiler_params=_cparams("parallel", "parallel"),
        name="adaln",
    )(cond, mod_w, mod_b.reshape(DEPTH, 1, six_d))


def _norm_mod(x, g, shift, scale):
    ms = jnp.mean(x * x, axis=-1, keepdims=True)
    h = x * lax.rsqrt(ms + EPS) * g
    return h * (1.0 + scale) + shift


def _rope(a, cos, sin_signed):
    lane = lax.broadcasted_iota(jnp.int32, a.shape, 1)
    first = (lane & (ROT_AXIS - 1)) < (ROT_AXIS // 2)
    partner = jnp.where(first, pltpu.roll(a, HEAD_DIM - ROT_AXIS // 2, axis=1),
                        pltpu.roll(a, ROT_AXIS // 2, axis=1))
    return a * cos + partner * sin_signed


def _pack_rows(h):
    half = h.shape[1] // 2
    lo = pltpu.bitcast(h[:, :half].astype(BF16).astype(F32), U32)
    hi = pltpu.bitcast(h[:, half:].astype(BF16).astype(F32), U32)
    return (lo >> 16) | (hi & jnp.uint32(0xFFFF0000))


def _unpack_rows(w):
    lo = pltpu.bitcast(w << 16, F32).astype(BF16)
    hi = pltpu.bitcast(w & jnp.uint32(0xFFFF0000), F32).astype(BF16)
    return jnp.concatenate([lo, hi], axis=1)


def _inproj_kernel(*refs, n_pool, n_q, qk_norm, rope):
    x_ref, g_ref, sh_ref, sc_ref, w_ref = refs[:5]
    pos = 5
    if rope:
        cos_ref, sin_ref = refs[pos:pos + 2]
        pos += 2
    if qk_norm:
        qg_ref, kg_ref = refs[pos:pos + 2]
        pos += 2
    outs = refs[pos:]
    if n_pool:
        u_ref, q_ref, k_ref, v_ref = outs
    else:
        q_ref, k_ref, v_ref = outs

    hb = _norm_mod(x_ref[0], g_ref[...], sh_ref[0], sc_ref[0]).astype(BF16)
    n_heads = w_ref.shape[1] // HEAD_DIM
    group = 4
    for c0 in range(0, n_heads, group):
        acc = jnp.dot(hb, w_ref[:, c0 * HEAD_DIM:(c0 + group) * HEAD_DIM], preferred_element_type=F32)
        for j in range(group):
            c = c0 + j
            a = acc[:, j * HEAD_DIM:(j + 1) * HEAD_DIM]
            if c < n_pool:
                u_ref[0, :, c * HEAD_DIM:(c + 1) * HEAD_DIM] = a
                continue
            c -= n_pool
            is_q = c < n_q
            is_k = (not is_q) and c < n_q + 4
            if is_q or is_k:
                if qk_norm:
                    gain = qg_ref[...] if is_q else kg_ref[...]
                    a = a * lax.rsqrt(jnp.mean(a * a, axis=-1, keepdims=True) + EPS) * gain
                if rope:
                    a = _rope(a, cos_ref[...], sin_ref[...])
            if is_q:
                q_ref[0, :, c * HEAD_DIM:(c + 1) * HEAD_DIM] = (a * Q_SCALE).astype(BF16)
            elif is_k:
                c -= n_q
                k_ref[0, :, c * HEAD_DIM:(c + 1) * HEAD_DIM] = a.astype(BF16)
            else:
                c -= n_q + 4
                v_ref[0, :, c * V_HEAD:c * V_HEAD + HEAD_DIM] = a.astype(BF16)
                v_ref[0, :, c * V_HEAD + HEAD_DIM:(c + 1) * V_HEAD] = jnp.ones(a.shape, BF16)


def _inproj(x, g, shift, scale, w, *, n_pool, n_q, tm, tables=None, qk_gains=None):
    B, T, D = x.shape
    per_batch = shift.shape[0] == B
    mod_map = (lambda b, t: (b, 0, 0)) if per_batch else (lambda b, t: (0, 0, 0))
    in_specs = [
        pl.BlockSpec((1, tm, D), lambda b, t: (b, t, 0)),
        pl.BlockSpec((1, D), lambda b, t: (0, 0)),
        pl.BlockSpec((1, 1, D), mod_map),
        pl.BlockSpec((1, 1, D), mod_map),
        pl.BlockSpec(w.shape, lambda b, t: (0, 0)),
    ]
    args = [x, g.reshape(1, D), shift, scale, w]
    if tables is not None:
        in_specs += [pl.BlockSpec((tm, HEAD_DIM), lambda b, t: (t, 0))] * 2
        args += list(tables)
    if qk_gains is not None:
        in_specs += [pl.BlockSpec((1, HEAD_DIM), lambda b, t: (0, 0))] * 2
        args += [qk_gains[0].reshape(1, HEAD_DIM), qk_gains[1].reshape(1, HEAD_DIM)]
    widths = ([(n_pool * HEAD_DIM, F32)] if n_pool else []) + [
        (n_q * HEAD_DIM, BF16), (KV_WIDTH, BF16), (V_WIDTH, BF16)]
    out_specs = [pl.BlockSpec((1, tm, wd), lambda b, t: (b, t, 0)) for wd, _ in widths]
    out_shape = [jax.ShapeDtypeStruct((B, T, wd), dt) for wd, dt in widths]
    return pl.pallas_call(
        functools.partial(_inproj_kernel, n_pool=n_pool, n_q=n_q,
                          qk_norm=qk_gains is not None, rope=tables is not None),
        grid=(B, T // tm),
        in_specs=in_specs, out_specs=out_specs, out_shape=out_shape,
        compiler_params=_cparams("parallel", "parallel"),
        name="inproj",
    )(*args)


def _pool_kernel(u_ref, w_ref, s_ref, o_ref, pad_ref, *, T, chunk):
    halo = 16
    pad_ref[0:halo, :] = jnp.zeros((halo, POOL_WIDTH), F32)
    pad_ref[halo + T:2 * halo + T, :] = jnp.zeros((halo, POOL_WIDTH), F32)
    pad_ref[halo:halo + T, :] = u_ref[0]
    for g, wdw in enumerate(POOL_WINDOWS):
        cols = slice(g * POOL_GROUP, (g + 1) * POOL_GROUP)
        half = wdw // 2
        for r0 in range(0, T, chunk):
            s = pad_ref[halo + r0 - half:halo + r0 - half + chunk, cols]
            for j in range(-half + 1, half):
                s = s + pad_ref[halo + r0 + j:halo + r0 + j + chunk, cols]
            t = r0 + lax.broadcasted_iota(jnp.int32, (chunk, 1), 0)
            cnt = (jnp.minimum(t + half, T) - jnp.maximum(t - half, 0)).astype(F32)
            d = (s / cnt - u_ref[0, r0:r0 + chunk, cols]).astype(BF16)
            y = jnp.dot(d, w_ref[g], preferred_element_type=F32) * s_ref[:, cols]
            o_ref[0, r0:r0 + chunk, cols] = y.astype(BF16)


def _pool(u, pool_w, pool_scale):
    B, T, _ = u.shape
    return pl.pallas_call(
        functools.partial(_pool_kernel, T=T, chunk=min(T, 512)),
        grid=(B,),
        in_specs=[
            pl.BlockSpec((1, T, POOL_WIDTH), lambda b: (b, 0, 0)),
            pl.BlockSpec(pool_w.shape, lambda b: (0, 0, 0)),
            pl.BlockSpec((1, POOL_WIDTH), lambda b: (0, 0)),
        ],
        out_specs=pl.BlockSpec((1, T, POOL_WIDTH), lambda b: (b, 0, 0)),
        out_shape=jax.ShapeDtypeStruct((B, T, POOL_WIDTH), BF16),
        scratch_shapes=[pltpu.VMEM((T + 32, POOL_WIDTH), F32)],
        compiler_params=_cparams("parallel"),
        name="pool",
    )(u, pool_w, pool_scale.reshape(1, POOL_WIDTH))


def _stack_heads(q_ref, h, G):
    return jnp.concatenate(
        [q_ref[0, :, (h * G + g) * HEAD_DIM:(h * G + g + 1) * HEAD_DIM] for g in range(G)], axis=0)


def _qk(q, k):
    return lax.dot_general(q, k, (((1,), (1,)), ((), ())), preferred_element_type=F32)


def _pv(p, v):
    return jnp.dot(p.astype(BF16), v, preferred_element_type=F32)


def _sink_column(sink_ref, h, G, tq):
    return jnp.concatenate([jnp.full((tq, 1), sink_ref[h * G + g] * LOG2E, F32) for g in range(G)], axis=0)


def _store_heads(o_ref, o, h, G, tq):
    for g in range(G):
        o_ref[0, :, (h * G + g) * HEAD_DIM:(h * G + g + 1) * HEAD_DIM] = o[g * tq:(g + 1) * tq].astype(BF16)


def _win_attn_kernel(sink_ref, q_ref, k_ref, v_ref, kc_ref, vc_ref, o_ref, *, T, tq):
    G = B_HEADS // B_KV_HEADS
    kw_len = tq + 2 * WINDOW
    q0 = pl.program_id(1) * tq
    start = pl.multiple_of(jnp.clip(q0 - WINDOW, 0, T - kw_len), WINDOW)
    delta = q0 - start
    r = lax.broadcasted_iota(jnp.int32, (G * tq, kw_len), 0) & (tq - 1)
    c = lax.broadcasted_iota(jnp.int32, (G * tq, kw_len), 1)
    band = jnp.abs(r + delta - c) <= WINDOW
    for h in range(B_KV_HEADS):
        hs = slice(h * HEAD_DIM, (h + 1) * HEAD_DIM)
        vs = slice(h * V_HEAD, (h + 1) * V_HEAD)
        qs = _stack_heads(q_ref, h, G)
        s_loc = jnp.where(band, _qk(qs, k_ref[0, pl.ds(start, kw_len), hs]), NEG_INF)
        s_ctx = _qk(qs, kc_ref[0, :, hs])
        sink = _sink_column(sink_ref, h, G, tq)
        m = jnp.maximum(jnp.maximum(jnp.max(s_loc, axis=-1, keepdims=True),
                                    jnp.max(s_ctx, axis=-1, keepdims=True)), sink)
        oa = (_pv(jnp.exp2(s_loc - m), v_ref[0, pl.ds(start, kw_len), vs])
              + _pv(jnp.exp2(s_ctx - m), vc_ref[0, :, vs]))
        o = oa[:, :HEAD_DIM] / (oa[:, HEAD_DIM:] + jnp.exp2(sink - m))
        _store_heads(o_ref, o, h, G, tq)


def _win_attn(sink, q, k, v, kc, vc, *, tq=128):
    B, T, qw = q.shape
    Tc = kc.shape[1]
    return pl.pallas_call(
        functools.partial(_win_attn_kernel, T=T, tq=tq),
        grid=(B, T // tq),
        in_specs=[
            pl.BlockSpec(memory_space=pltpu.SMEM),
            pl.BlockSpec((1, tq, qw), lambda b, t: (b, t, 0)),
            pl.BlockSpec((1, T, KV_WIDTH), lambda b, t: (b, 0, 0)),
            pl.BlockSpec((1, T, V_WIDTH), lambda b, t: (b, 0, 0)),
            pl.BlockSpec((1, Tc, KV_WIDTH), lambda b, t: (b, 0, 0)),
            pl.BlockSpec((1, Tc, V_WIDTH), lambda b, t: (b, 0, 0)),
        ],
        out_specs=pl.BlockSpec((1, tq, qw), lambda b, t: (b, t, 0)),
        out_shape=jax.ShapeDtypeStruct((B, T, qw), BF16),
        compiler_params=_cparams("parallel", "arbitrary"),
        name="win_attn",
    )(sink, q, k, v, kc, vc)


def _ctx_attn_kernel(*refs, G, has_sink):
    if has_sink:
        sink_ref, q_ref, k_ref, v_ref, o_ref = refs
    else:
        q_ref, k_ref, v_ref, o_ref = refs
    tq = q_ref.shape[1]
    for h in range(q_ref.shape[2] // (G * HEAD_DIM)):
        qs = _stack_heads(q_ref, h, G)
        s = _qk(qs, k_ref[0, :, h * HEAD_DIM:(h + 1) * HEAD_DIM])
        m = jnp.max(s, axis=-1, keepdims=True)
        if has_sink:
            sink = _sink_column(sink_ref, h, G, tq)
            m = jnp.maximum(m, sink)
        oa = _pv(jnp.exp2(s - m), v_ref[0, :, h * V_HEAD:(h + 1) * V_HEAD])
        den = oa[:, HEAD_DIM:]
        if has_sink:
            den = den + jnp.exp2(sink - m)
        _store_heads(o_ref, oa[:, :HEAD_DIM] / den, h, G, tq)


def _ctx_attn(q, k, v, *, G, sink=None):
    B, Tc, qw = q.shape
    in_specs, args = [], []
    if sink is not None:
        in_specs.append(pl.BlockSpec(memory_space=pltpu.SMEM))
        args.append(sink)
    in_specs += [pl.BlockSpec((1, Tc, a.shape[2]), lambda b: (b, 0, 0)) for a in (q, k, v)]
    return pl.pallas_call(
        functools.partial(_ctx_attn_kernel, G=G, has_sink=sink is not None),
        grid=(B,),
        in_specs=in_specs,
        out_specs=pl.BlockSpec((1, Tc, qw), lambda b: (b, 0, 0)),
        out_shape=jax.ShapeDtypeStruct((B, Tc, qw), BF16),
        compiler_params=_cparams("parallel"),
        name="ctx_attn",
    )(*args, q, k, v)


def _lane_tile_max(s):
    parts = [s[:, t * HEAD_DIM:(t + 1) * HEAD_DIM] for t in range(s.shape[1] // HEAD_DIM)]
    while len(parts) > 1:
        parts = [jnp.maximum(parts[i], parts[i + 1]) if i + 1 < len(parts) else parts[i]
                 for i in range(0, len(parts), 2)]
    return parts[0]


def _full_attn_kernel(q_ref, kc_ref, vc_ref, k_ref, v_ref, o_ref, s_ref, p_ref, m_ref, *, G, row_block, key_chunk):
    tq = q_ref.shape[1]
    R = G * tq
    half = R // 2
    Tc, Tk = kc_ref.shape[1], s_ref.shape[2]
    n_lane_tiles = Tk // HEAD_DIM
    n_heads = q_ref.shape[2] // (G * HEAD_DIM)

    def scores(h):
        hs = slice(h * HEAD_DIM, (h + 1) * HEAD_DIM)
        qs = _stack_heads(q_ref, h, G)
        for r0 in (0, half):
            q_half = qs[r0:r0 + half]
            s = _qk(q_half, kc_ref[0, :, hs])
            s_ref[h % 2, r0:r0 + half, 0:Tc] = s
            m_run = _lane_tile_max(s)
            for c0 in range(0, Tk - Tc, key_chunk):
                s = _qk(q_half, k_ref[0, c0:c0 + key_chunk, hs])
                s_ref[h % 2, r0:r0 + half, Tc + c0:Tc + c0 + key_chunk] = s
                m_run = jnp.maximum(m_run, _lane_tile_max(s))
            m_ref[h % 2, r0:r0 + half, :] = jnp.broadcast_to(
                jnp.max(m_run, axis=-1, keepdims=True), (half, HEAD_DIM))

    def weights(h):
        for r0 in range(0, R, row_block):
            mb = m_ref[h % 2, r0:r0 + row_block, :]
            for t in range(n_lane_tiles):
                cols = slice(t * HEAD_DIM, (t + 1) * HEAD_DIM)
                p_ref[h % 2, r0:r0 + row_block, cols] = jnp.exp2(
                    s_ref[h % 2, r0:r0 + row_block, cols] - mb).astype(BF16)

    def values(h):
        vs = slice(h * V_HEAD, (h + 1) * V_HEAD)
        for r0 in (0, half):
            oa = (jnp.dot(p_ref[h % 2, r0:r0 + half, 0:Tc], vc_ref[0, :, vs], preferred_element_type=F32)
                  + jnp.dot(p_ref[h % 2, r0:r0 + half, Tc:Tk], v_ref[0, :, vs], preferred_element_type=F32))
            o = oa[:, :HEAD_DIM] / oa[:, HEAD_DIM:]
            for g in range(G):
                lo, hi = g * tq, (g + 1) * tq
                a, bnd = max(lo, r0), min(hi, r0 + half)
                if a < bnd:
                    o_ref[0, a - lo:bnd - lo, (h * G + g) * HEAD_DIM:(h * G + g + 1) * HEAD_DIM] = (
                        o[a - r0:bnd - r0].astype(BF16))

    scores(0)
    for h in range(n_heads):
        if h + 1 < n_heads:
            scores(h + 1)
        weights(h)
        values(h)


def _full_attn(q, kc, vc, k, v, *, G, tq):
    B, T, qw = q.shape
    Tc = kc.shape[1]
    Tk = Tc + T
    return pl.pallas_call(
        functools.partial(_full_attn_kernel, G=G, row_block=32, key_chunk=512),
        grid=(B, T // tq),
        in_specs=[
            pl.BlockSpec((1, tq, qw), lambda b, t: (b, t, 0)),
            pl.BlockSpec((1, Tc, KV_WIDTH), lambda b, t: (b, 0, 0), pipeline_mode=pl.Buffered(1)),
            pl.BlockSpec((1, Tc, V_WIDTH), lambda b, t: (b, 0, 0), pipeline_mode=pl.Buffered(1)),
            pl.BlockSpec((1, T, KV_WIDTH), lambda b, t: (b, 0, 0), pipeline_mode=pl.Buffered(1)),
            pl.BlockSpec((1, T, V_WIDTH), lambda b, t: (b, 0, 0), pipeline_mode=pl.Buffered(1)),
        ],
        out_specs=pl.BlockSpec((1, tq, qw), lambda b, t: (b, t, 0)),
        out_shape=jax.ShapeDtypeStruct((B, T, qw), BF16),
        scratch_shapes=[pltpu.VMEM((2, G * tq, Tk), F32), pltpu.VMEM((2, G * tq, Tk), BF16),
                        pltpu.VMEM((2, G * tq, HEAD_DIM), F32)],
        compiler_params=_cparams("parallel", "arbitrary"),
        name="full_attn",
    )(q, kc, vc, k, v)


def _split3(a):
    hi = a.astype(BF16)
    r1 = a - hi.astype(F32)
    mid = r1.astype(BF16)
    lo = (r1 - mid.astype(F32)).astype(BF16)
    return hi, mid, lo


def _outproj_kernel(*refs, n_y):
    y_refs = refs[:n_y]
    (w_ref, x_ref, g1_ref, ng_ref, sh_ref, sc_ref, rw_ref, xo_ref, h_ref, lg_ref) = refs[n_y:]
    acc = None
    row = 0
    for y_ref in y_refs:
        wd = y_ref.shape[2]
        part = jnp.dot(y_ref[0], w_ref[row:row + wd, :], preferred_element_type=F32)
        acc = part if acc is None else acc + part
        row += wd
    x = x_ref[0] + g1_ref[0] * acc
    xo_ref[0] = x
    h = _norm_mod(x, ng_ref[...], sh_ref[0], sc_ref[0])
    h_ref[0] = _pack_rows(h)
    h_hi, h_mid, h_lo = _split3(h)
    E = N_EXPERTS
    dot = functools.partial(jnp.dot, preferred_element_type=F32)
    p_hi = dot(h_hi, rw_ref[...])
    p_mid = dot(h_mid, rw_ref[:, :2 * E])
    p_lo = dot(h_lo, rw_ref[:, :E])
    lg_ref[0] = (p_hi[:, :E] + (p_hi[:, E:2 * E] + p_mid[:, :E])
                 + (p_hi[:, 2 * E:] + p_mid[:, E:] + p_lo))


def _outproj(ys, w, x, g1, ng, shift, scale, rw3, *, tm):
    B, T, D = x.shape
    per_batch = g1.shape[0] == B
    mod_map = (lambda b, t: (b, 0, 0)) if per_batch else (lambda b, t: (0, 0, 0))
    tile = lambda wd: pl.BlockSpec((1, tm, wd), lambda b, t: (b, t, 0))
    in_specs = [tile(y.shape[2]) for y in ys] + [
        pl.BlockSpec(w.shape, lambda b, t: (0, 0)),
        tile(D),
        pl.BlockSpec((1, 1, D), mod_map),
        pl.BlockSpec((1, D), lambda b, t: (0, 0)),
        pl.BlockSpec((1, 1, D), mod_map),
        pl.BlockSpec((1, 1, D), mod_map),
        pl.BlockSpec(rw3.shape, lambda b, t: (0, 0)),
    ]
    return pl.pallas_call(
        functools.partial(_outproj_kernel, n_y=len(ys)),
        grid=(B, T // tm),
        in_specs=in_specs,
        out_specs=[tile(D), tile(D // 2), tile(N_EXPERTS)],
        out_shape=[jax.ShapeDtypeStruct((B, T, D), F32),
                   jax.ShapeDtypeStruct((B, T, D // 2), U32),
                   jax.ShapeDtypeStruct((B, T, N_EXPERTS), F32)],
        compiler_params=_cparams("parallel", "parallel"),
        name="outproj",
    )(*ys, w, x, g1, ng.reshape(1, D), shift, scale, rw3)


def _token_prefix(x, tri):
    T = x.shape[0]
    xb = x.astype(BF16)
    local = [jnp.dot(tri, xb[j:j + LANES], preferred_element_type=F32) for j in range(0, T, LANES)]
    out, run = [], None
    for blk in local:
        blk = blk if run is None else blk + run
        run = blk[LANES - 1:LANES, :]
        out.append(blk)
    return jnp.concatenate(out, axis=0)


def _route_kernel(lg_ref, idx_ref, gate_ref, sel_ref, slot_ref, lo_ref, aff_ref, pos_ref, *, T, cap, tile):
    E = N_EXPERTS
    lg = lg_ref[0]
    ex = jnp.exp(lg - jnp.max(lg, axis=-1, keepdims=True))
    aff = ex / jnp.sum(ex, axis=-1, keepdims=True)
    bits = pltpu.bitcast(aff, I32)

    def bisect(_, c):
        lo, hi = c
        mid = lo + ((hi - lo) >> 1)
        cnt = jnp.sum(jnp.where(bits >= mid, 1.0, 0.0), axis=0, keepdims=True)
        ge = cnt >= cap
        return jnp.where(ge, mid, lo), jnp.where(ge, hi, mid)

    thr, _ = lax.fori_loop(0, 31, bisect, (jnp.zeros((1, E), I32), jnp.full((1, E), 0x3F800001, I32)))
    gt = bits > thr
    eq = bits == thr
    r = lax.broadcasted_iota(I32, (LANES, LANES), 0)
    cidx = lax.broadcasted_iota(I32, (LANES, LANES), 1)
    tri = jnp.where(cidx <= r, 1.0, 0.0).astype(BF16)
    eq_f = jnp.where(eq, 1.0, 0.0)
    need = cap - jnp.sum(jnp.where(gt, 1.0, 0.0), axis=0, keepdims=True)
    tie_rank = _token_prefix(eq_f, tri) - eq_f
    sel = gt | (eq & (tie_rank < need))
    sel_f = jnp.where(sel, 1.0, 0.0)
    pos = _token_prefix(sel_f, tri)
    sel_ref[0] = sel_f
    slot_ref[0] = pos - 1.0
    lo_ref[0, 0:1, :] = jnp.zeros((1, E), I32)
    for j in range(1, T // tile + 1):
        lo_ref[0, j:j + 1, :] = pos[j * tile - 1:j * tile, :].astype(I32)
    aff_ref[...] = aff
    pos_ref[...] = jnp.where(sel, pos, 0.0)

    chunk = min(T, 256)
    n_lt = (cap + LANES - 1) // LANES
    for e in range(E):
        def body(ci, accs):
            rows = pl.ds(pl.multiple_of(ci * chunk, chunk), chunk)
            p_col = jnp.broadcast_to(pos_ref[rows, e:e + 1], (chunk, LANES))
            a_col = jnp.broadcast_to(aff_ref[rows, e:e + 1], (chunk, LANES))
            tok = (ci * chunk + lax.broadcasted_iota(I32, (chunk, LANES), 0)).astype(F32)
            out = []
            for lt in range(n_lt):
                want = (lax.broadcasted_iota(I32, (chunk, LANES), 1) + (lt * LANES + 1)).astype(F32)
                hit = p_col == want
                ti = jnp.where(hit, tok, 0.0).reshape(chunk // 8, 8, LANES).sum(axis=0)
                gi = jnp.where(hit, a_col, 0.0).reshape(chunk // 8, 8, LANES).sum(axis=0)
                out += [accs[2 * lt] + ti, accs[2 * lt + 1] + gi]
            return tuple(out)

        accs = lax.fori_loop(0, T // chunk, body, tuple(jnp.zeros((8, LANES), F32) for _ in range(2 * n_lt)))
        for lt in range(n_lt):
            w = min(LANES, cap - lt * LANES)
            idx_ref[0, e:e + 1, lt * LANES:lt * LANES + w] = (
                jnp.sum(accs[2 * lt], axis=0, keepdims=True)[:, :w].astype(I32))
            gate_ref[0, e:e + 1, lt * LANES:lt * LANES + w] = jnp.sum(accs[2 * lt + 1], axis=0, keepdims=True)[:, :w]


def _route(logits, *, cap, tile):
    B, T, E = logits.shape
    nb = T // tile + 1
    tok_spec = pl.BlockSpec((1, T, E), lambda b: (b, 0, 0))
    slot_spec = pl.BlockSpec((1, E, cap), lambda b: (b, 0, 0))
    return pl.pallas_call(
        functools.partial(_route_kernel, T=T, cap=cap, tile=tile),
        grid=(B,),
        in_specs=[tok_spec],
        out_specs=[slot_spec, slot_spec, tok_spec, tok_spec, pl.BlockSpec((1, nb, E), lambda b: (b, 0, 0))],
        out_shape=[jax.ShapeDtypeStruct((B, E, cap), I32), jax.ShapeDtypeStruct((B, E, cap), F32),
                   jax.ShapeDtypeStruct((B, T, E), F32), jax.ShapeDtypeStruct((B, T, E), F32),
                   jax.ShapeDtypeStruct((B, nb, E), I32)],
        scratch_shapes=[pltpu.VMEM((T, E), F32), pltpu.VMEM((T, E), F32)],
        compiler_params=_cparams("arbitrary"),
        name="route",
    )(logits)


def _expert_kernel(idx_ref, h_hbm, gate_ref, wg_ref, wu_ref, wd_ref, o_ref, rows_ref, xs_ref, acc_ref, sem,
                   *, tm, n_tiles):
    n, f = pl.program_id(0), pl.program_id(1)
    nf = pl.num_programs(1)
    per = tm // nf
    slot = lax.rem(n, 2)

    def row_copy(tile, r, buf):
        return pltpu.make_async_copy(h_hbm.at[pl.ds(idx_ref[tile * tm + r], 1), :],
                                     rows_ref.at[buf, pl.ds(r, 1), :], sem.at[buf])

    def tile_wait(buf):
        pltpu.make_async_copy(h_hbm.at[pl.ds(0, tm), :], rows_ref.at[buf], sem.at[buf]).wait()

    @pl.when((n == 0) & (f == 0))
    def _():
        for r in range(tm):
            row_copy(0, r, 0).start()

    @pl.when(f == 0)
    def _():
        tile_wait(slot)
        xs_ref[...] = _unpack_rows(rows_ref[slot])
        acc_ref[...] = jnp.zeros(acc_ref.shape, F32)

    nxt = jnp.minimum(n + 1, n_tiles - 1)
    for i in range(per):
        row_copy(nxt, f * per + i, 1 - slot).start()

    xs = xs_ref[...]
    a = jnp.dot(xs, wg_ref[0, 0].astype(BF16), preferred_element_type=F32)
    u = jnp.dot(xs, wu_ref[0, 0].astype(BF16), preferred_element_type=F32)
    act = (a * (1.0 / (1.0 + jnp.exp(-a))) * u).astype(BF16)
    acc_ref[...] += jnp.dot(act, wd_ref[0, 0].astype(BF16), preferred_element_type=F32)

    @pl.when(f == nf - 1)
    def _():
        blk = min(tm, LANES)
        eye = lax.broadcasted_iota(I32, (blk, blk), 0) == lax.broadcasted_iota(I32, (blk, blk), 1)
        for r0 in range(0, tm, blk):
            g_row = jnp.broadcast_to(gate_ref[0, :, r0:r0 + blk], (blk, blk))
            g_col = jnp.sum(jnp.where(eye, g_row, 0.0), axis=-1, keepdims=True)
            o_ref[r0:r0 + blk, :] = (acc_ref[r0:r0 + blk, :] * g_col).astype(BF16)

    @pl.when((n == n_tiles - 1) & (f == nf - 1))
    def _():
        tile_wait(1 - slot)


def _experts(idx, h, gate, wg, wu, wd, layer, *, tm, tf=256):
    N, half = h.shape
    D = 2 * half
    _, E, _, F = wg.shape
    n_tiles = idx.shape[0] // tm
    per_e = n_tiles // E
    nf = F // tf
    assert tm % nf == 0
    return pl.pallas_call(
        functools.partial(_expert_kernel, tm=tm, n_tiles=n_tiles),
        grid_spec=pltpu.PrefetchScalarGridSpec(
            num_scalar_prefetch=1,
            grid=(n_tiles, nf),
            in_specs=[
                pl.BlockSpec(memory_space=pl.ANY),
                pl.BlockSpec((1, 1, tm), lambda n, f, idx: (n, 0, 0)),
                pl.BlockSpec((1, 1, D, tf), lambda n, f, idx: (layer, n // per_e, 0, f)),
                pl.BlockSpec((1, 1, D, tf), lambda n, f, idx: (layer, n // per_e, 0, f)),
                pl.BlockSpec((1, 1, tf, D), lambda n, f, idx: (layer, n // per_e, f, 0)),
            ],
            out_specs=pl.BlockSpec((tm, D), lambda n, f, idx: (n, 0)),
            scratch_shapes=[pltpu.VMEM((2, tm, half), U32), pltpu.VMEM((tm, D), BF16), pltpu.VMEM((tm, D), F32),
                            pltpu.SemaphoreType.DMA((2,))]),
        out_shape=jax.ShapeDtypeStruct((n_tiles * tm, D), BF16),
        compiler_params=_cparams("arbitrary", "arbitrary"),
        name="experts",
    )(idx, h, gate, wg, wu, wd)


def _combine_kernel(lo_ref, x_ref, g_ref, sel_ref, slot_ref, ye_hbm, *rest, cap, win, n_tiles, seg_stride, final):
    if final:
        fg_ref, o_ref, stage_ref, extra_ref, hot_ref, sem = rest
    else:
        o_ref, stage_ref, extra_ref, hot_ref, sem = rest
    E = N_EXPERTS
    b, t = pl.program_id(0), pl.program_id(1)
    step = b * n_tiles + t
    n_steps = pl.num_programs(0) * n_tiles
    buf = lax.rem(step, 2)
    tm = x_ref.shape[1]

    def seg_lo(bb, tt, e):
        return lo_ref[(bb * (n_tiles + 1) + tt) * E + e]

    def win_start(bb, tt, e):
        w0 = jnp.minimum((seg_lo(bb, tt, e) // 16) * 16, cap - win)
        return pl.multiple_of((e * seg_stride + bb) * cap + w0, 16), w0

    def window_copy(bb, tt, e, k):
        row, _ = win_start(bb, tt, e)
        return pltpu.make_async_copy(ye_hbm.at[pl.ds(row, win), :], stage_ref.at[k, pl.ds(e * win, win), :],
                                     sem.at[k])

    @pl.when(step == 0)
    def _():
        for e in range(E):
            window_copy(b, t, e, 0).start()

    @pl.when(step + 1 < n_steps)
    def _():
        nb = jnp.where(t + 1 < n_tiles, b, b + 1)
        nt = jnp.where(t + 1 < n_tiles, t + 1, 0)
        for e in range(E):
            window_copy(nb, nt, e, 1 - buf).start()

    for e in range(E):
        window_copy(b, t, e, buf).wait()

    sel = sel_ref[0] > 0.5
    slot = slot_ref[0]
    lane = lax.broadcasted_iota(I32, (tm, win), 1).astype(F32)
    for e in range(E):
        _, w0 = win_start(b, t, e)
        rel = jnp.where(sel[:, e:e + 1], slot[:, e:e + 1] - w0.astype(F32), -1.0)
        hot_ref[:, e * win:(e + 1) * win] = jnp.where(
            jnp.broadcast_to(rel, (tm, win)) == lane, 1.0, 0.0).astype(BF16)
    D = x_ref.shape[2]
    cw = min(D, 512)
    for c0 in range(0, D, cw):
        cols = slice(c0, c0 + cw)
        moe = jnp.dot(hot_ref[...], stage_ref[buf, :, cols], preferred_element_type=F32)
        o_ref[0, :, cols] = x_ref[0, :, cols] + g_ref[0, :, cols] * moe

    if win < cap:
        for e in range(E):
            _, w0 = win_start(b, t, e)
            hi = seg_lo(b, t + 1, e)
            n_extra = jnp.maximum(hi - (w0 + win) + win - 1, 0) // win

            def extra(k, carry, e=e, w0=w0):
                c0 = w0 + k * win
                ck = jnp.minimum(c0, cap - win)
                row = pl.multiple_of((e * seg_stride + b) * cap + ck, 16)
                cp = pltpu.make_async_copy(ye_hbm.at[pl.ds(row, win), :], extra_ref, sem.at[2])
                cp.start()
                cp.wait()
                s_e = slot[:, e:e + 1]
                ok = sel[:, e:e + 1] & (s_e >= c0.astype(F32)) & (s_e < (c0 + win).astype(F32))
                rel = jnp.where(ok, s_e - ck.astype(F32), -1.0)
                h1 = jnp.where(jnp.broadcast_to(rel, (tm, win)) == lane, 1.0, 0.0).astype(BF16)
                o_ref[0] += g_ref[0] * jnp.dot(h1, extra_ref[...], preferred_element_type=F32)
                return carry

            lax.fori_loop(1, n_extra + 1, extra, 0)

    if final:
        rb = min(tm, 64)

        def norm_rows(i, carry):
            rows = pl.ds(pl.multiple_of(i * rb, rb), rb)
            y = o_ref[0, rows, :]
            o_ref[0, rows, :] = y * lax.rsqrt(jnp.mean(y * y, axis=-1, keepdims=True) + EPS) * fg_ref[...]
            return carry

        lax.fori_loop(0, tm // rb, norm_rows, 0)


def _combine(lo, x, g2, sel, slot, ye, *, cap, tile, seg_stride, final_g=None):
    B, T, D = x.shape
    n_tiles = T // tile
    win = min(cap, LANES)
    per_batch = g2.shape[0] == B
    mod_map = (lambda b, t, lo: (b, 0, 0)) if per_batch else (lambda b, t, lo: (0, 0, 0))
    tok = lambda w: pl.BlockSpec((1, tile, w), lambda b, t, lo: (b, t, 0))
    in_specs = [tok(D), pl.BlockSpec((1, 1, D), mod_map), tok(N_EXPERTS), tok(N_EXPERTS),
                pl.BlockSpec(memory_space=pl.ANY)]
    args = [x, g2, sel, slot, ye]
    if final_g is not None:
        in_specs.append(pl.BlockSpec((1, D), lambda b, t, lo: (0, 0)))
        args.append(final_g.reshape(1, D))
    return pl.pallas_call(
        functools.partial(_combine_kernel, cap=cap, win=win, n_tiles=n_tiles, seg_stride=seg_stride,
                          final=final_g is not None),
        grid_spec=pltpu.PrefetchScalarGridSpec(
            num_scalar_prefetch=1,
            grid=(B, n_tiles),
            in_specs=in_specs,
            out_specs=tok(D),
            scratch_shapes=[pltpu.VMEM((2, N_EXPERTS * win, D), BF16), pltpu.VMEM((win, D), BF16),
                            pltpu.VMEM((tile, N_EXPERTS * win), BF16), pltpu.SemaphoreType.DMA((3,))]),
        out_shape=jax.ShapeDtypeStruct((B, T, D), F32),
        compiler_params=_cparams("arbitrary", "arbitrary"),
        name="combine",
    )(lo.reshape(-1), *args)


def _rope_tables(n_tokens):
    rows = n_tokens // GRID_W
    row = jnp.broadcast_to(jnp.arange(rows)[:, None], (rows, GRID_W)).reshape(-1).astype(F32)
    col = jnp.broadcast_to(jnp.arange(GRID_W)[None, :], (rows, GRID_W)).reshape(-1).astype(F32)
    inv = ROPE_THETA ** (-jnp.arange(0, ROT_AXIS, 2, dtype=F32) / ROT_AXIS)
    ang_r, ang_c = row[:, None] * inv, col[:, None] * inv
    cos = jnp.concatenate([jnp.cos(ang_r)] * 2 + [jnp.cos(ang_c)] * 2, axis=1)
    sin = jnp.concatenate([-jnp.sin(ang_r), jnp.sin(ang_r), -jnp.sin(ang_c), jnp.sin(ang_c)], axis=1)
    return cos, sin


def _moe(h, logits, x, g2, wg, wu, wd, layer, *, tile, final_g=None):
    B, T, D = x.shape
    cap = CAPACITY_FACTOR * T // N_EXPERTS
    idx, gate, sel, slot, lo = _route(logits, cap=cap, tile=tile)
    rows = jnp.swapaxes(idx + (jnp.arange(B, dtype=I32) * T)[:, None, None], 0, 1).reshape(-1)
    tm = min(B * cap, 1024)
    gate = jnp.swapaxes(gate, 0, 1).reshape(-1, 1, tm)
    ye = _experts(rows, h.reshape(B * T, D // 2), gate, wg, wu, wd, layer, tm=tm)
    return _combine(lo, x, g2, sel, slot, ye, cap=cap, tile=tile, seg_stride=B, final_g=final_g)


def kernel(x, c, ctx, c_ctx, mod_w, mod_b, norm_g, w_out, router_w, expert_w_gate, expert_w_up,
           expert_w_down, ab_w_in, pool_w, pool_scale, ab_sink, c_w_in, c_q_norm_g, c_k_norm_g,
           final_norm_g):
    B, T, D = x.shape
    Tc = ctx.shape[1]
    tables = _rope_tables(T)

    cond = jnp.zeros((COND_ROWS, D), F32).at[:B].set(c).at[B].set(c_ctx)
    mod = _adaln(cond, mod_w, mod_b)

    for i in range(DEPTH):
        last = i == DEPTH - 1
        j = i // 2
        mx = [mod[i, :B, k * D:(k + 1) * D].reshape(B, 1, D) for k in range(6)]
        mc = [mod[i, B:B + 1, k * D:(k + 1) * D].reshape(1, 1, D) for k in range(6)]
        rw3 = jnp.concatenate(_split3(router_w[i]), axis=1)
        wo = w_out[i].astype(BF16)
        wg, wu, wd = expert_w_gate, expert_w_up, expert_w_down
        if i % 2 == 0:
            w_in = ab_w_in[j].astype(BF16)
            u, q, k, v = _inproj(x, norm_g[i, 0], mx[0], mx[1], w_in, n_pool=4, n_q=B_HEADS,
                                 tm=512, tables=tables)
            uc, qc, kc, vc = _inproj(ctx, norm_g[i, 0], mc[0], mc[1], w_in, n_pool=4, n_q=B_HEADS, tm=Tc)
            pw = pool_w[j].astype(BF16)
            ys = [_pool(u, pw, pool_scale[j]), _win_attn(ab_sink[j], q, k, v, kc, vc)]
            if not last:
                ycs = [_pool(uc, pw, pool_scale[j]),
                       _ctx_attn(qc, kc, vc, G=B_HEADS // B_KV_HEADS, sink=ab_sink[j])]
        else:
            w_in = c_w_in[j].astype(BF16)
            gains = (c_q_norm_g[j], c_k_norm_g[j])
            q, k, v = _inproj(x, norm_g[i, 0], mx[0], mx[1], w_in, n_pool=0, n_q=C_HEADS,
                              tm=512, tables=tables, qk_gains=gains)
            qc, kc, vc = _inproj(ctx, norm_g[i, 0], mc[0], mc[1], w_in, n_pool=0, n_q=C_HEADS,
                                 tm=Tc, qk_gains=gains)
            ys = [_full_attn(q, kc, vc, k, v, G=C_HEADS // C_KV_HEADS, tq=128)]
            if not last:
                ycs = [_ctx_attn(qc, kc, vc, G=C_HEADS // C_KV_HEADS)]

        x, h2, lg = _outproj(ys, wo, x, mx[2], norm_g[i, 1], mx[3], mx[4], rw3, tm=512)
        x = _moe(h2, lg, x, mx[5], wg, wu, wd, i, tile=512, final_g=final_norm_g if last else None)
        if not last:
            ctx, h2c, lgc = _outproj(ycs, wo, ctx, mc[2], norm_g[i, 1], mc[3], mc[4], rw3, tm=Tc)
            ctx = _moe(h2c, lgc, ctx, mc[5], wg, wu, wd, i, tile=Tc)

    return x
```

```python
import functools

import jax
import jax.numpy as jnp
from jax import lax
from jax.experimental import pallas as pl
from jax.experimental.pallas import tpu as pltpu

F32 = jnp.float32
BF16 = jnp.bfloat16
I32 = jnp.int32
U32 = jnp.uint32
LANES = 128

D_MODEL = 2048
DEPTH = 2
GRID_W = 64
HEAD_DIM = 128
EPS = 1e-6
NEG_INF = -1e30
ATTN_SCALE = HEAD_DIM ** -0.5
POOL_WINDOWS = (2, 4, 8, 16)
POOL_GROUP = 128
POOL_WIDTH = 512
WINDOW = 128
B_HEADS = 12
B_KV_HEADS = 4
C_HEADS = 16
C_KV_HEADS = 4
KV_WIDTH = 4 * HEAD_DIM
V_HEAD = 2 * HEAD_DIM
V_WIDTH = 4 * V_HEAD
LOG2E = 1.4426950408889634
Q_SCALE = ATTN_SCALE * LOG2E
ROT_AXIS = HEAD_DIM // 2
ROPE_THETA = 10000.0
N_EXPERTS = 16
CAPACITY_FACTOR = 2
COND_ROWS = 16
VMEM_LIMIT = 56 * 1024 * 1024
TOKEN_TILE = 512
ADALN_COLS = 1024
FULL_ATTN_TQ = 128
WIN_ATTN_TQ = 256
EXPERT_ROWS = 1024
EXPERT_FF = 256


def _cparams(*sem):
    return pltpu.CompilerParams(dimension_semantics=sem, vmem_limit_bytes=VMEM_LIMIT)


def _adaln_kernel(cond_ref, w_ref, b_ref, o_ref):
    cnd = cond_ref[...]
    s = cnd * (1.0 / (1.0 + jnp.exp(-cnd)))
    o_ref[0] = jnp.dot(s.astype(BF16), w_ref[0].astype(BF16), preferred_element_type=F32) + b_ref[0]


def _adaln(cond, mod_w, mod_b):
    tn = ADALN_COLS
    six_d = mod_w.shape[-1]
    return pl.pallas_call(
        _adaln_kernel,
        grid=(DEPTH, six_d // tn),
        in_specs=[
            pl.BlockSpec((COND_ROWS, D_MODEL), lambda i, j: (0, 0)),
            pl.BlockSpec((1, D_MODEL, tn), lambda i, j: (i, 0, j)),
            pl.BlockSpec((1, 1, tn), lambda i, j: (i, 0, j)),
        ],
        out_specs=pl.BlockSpec((1, COND_ROWS, tn), lambda i, j: (i, 0, j)),
        out_shape=jax.ShapeDtypeStruct((DEPTH, COND_ROWS, six_d), F32),
        compiler_params=_cparams("parallel", "parallel"),
        name="adaln",
    )(cond, mod_w, mod_b.reshape(DEPTH, 1, six_d))


def _norm_mod(x, g, shift, scale):
    ms = jnp.mean(x * x, axis=-1, keepdims=True)
    h = x * lax.rsqrt(ms + EPS) * g
    return h * (1.0 + scale) + shift


def _rope(a, cos, sin_signed):
    lane = lax.broadcasted_iota(jnp.int32, a.shape, 1)
    first = (lane & (ROT_AXIS - 1)) < (ROT_AXIS // 2)
    partner = jnp.where(first, pltpu.roll(a, HEAD_DIM - ROT_AXIS // 2, axis=1),
                        pltpu.roll(a, ROT_AXIS // 2, axis=1))
    return a * cos + partner * sin_signed


def _pack_rows(h):
    half = h.shape[1] // 2
    lo = pltpu.bitcast(h[:, :half].astype(BF16).astype(F32), U32)
    hi = pltpu.bitcast(h[:, half:].astype(BF16).astype(F32), U32)
    return (lo >> 16) | (hi & jnp.uint32(0xFFFF0000))


def _unpack_rows(w):
    lo = pltpu.bitcast(w << 16, F32).astype(BF16)
    hi = pltpu.bitcast(w & jnp.uint32(0xFFFF0000), F32).astype(BF16)
    return jnp.concatenate([lo, hi], axis=1)


def _inproj_kernel(*refs, n_pool, n_q, qk_norm, rope):
    x_ref, g_ref, sh_ref, sc_ref, w_ref = refs[:5]
    pos = 5
    if rope:
        cos_ref, sin_ref = refs[pos:pos + 2]
        pos += 2
    if qk_norm:
        qg_ref, kg_ref = refs[pos:pos + 2]
        pos += 2
    outs = refs[pos:]
    if n_pool:
        u_ref, q_ref, k_ref, v_ref = outs
    else:
        q_ref, k_ref, v_ref = outs

    hb = _norm_mod(x_ref[0], g_ref[...], sh_ref[0], sc_ref[0]).astype(BF16)
    n_heads = w_ref.shape[1] // HEAD_DIM
    group = 4
    for c0 in range(0, n_heads, group):
        acc = jnp.dot(hb, w_ref[:, c0 * HEAD_DIM:(c0 + group) * HEAD_DIM], preferred_element_type=F32)
        for j in range(group):
            c = c0 + j
            a = acc[:, j * HEAD_DIM:(j + 1) * HEAD_DIM]
            if c < n_pool:
                u_ref[0, :, c * HEAD_DIM:(c + 1) * HEAD_DIM] = a
                continue
            c -= n_pool
            is_q = c < n_q
            is_k = (not is_q) and c < n_q + 4
            if is_q or is_k:
                if qk_norm:
                    gain = qg_ref[...] if is_q else kg_ref[...]
                    a = a * lax.rsqrt(jnp.mean(a * a, axis=-1, keepdims=True) + EPS) * gain
                if rope:
                    a = _rope(a, cos_ref[...], sin_ref[...])
            if is_q:
                q_ref[0, :, c * HEAD_DIM:(c + 1) * HEAD_DIM] = (a * Q_SCALE).astype(BF16)
            elif is_k:
                c -= n_q
                k_ref[0, :, c * HEAD_DIM:(c + 1) * HEAD_DIM] = a.astype(BF16)
            else:
                c -= n_q + 4
                v_ref[0, :, c * V_HEAD:c * V_HEAD + HEAD_DIM] = a.astype(BF16)
                v_ref[0, :, c * V_HEAD + HEAD_DIM:(c + 1) * V_HEAD] = jnp.ones(a.shape, BF16)


def _inproj(x, g, shift, scale, w, *, n_pool, n_q, tm, tables=None, qk_gains=None):
    B, T, D = x.shape
    per_batch = shift.shape[0] == B
    mod_map = (lambda b, t: (b, 0, 0)) if per_batch else (lambda b, t: (0, 0, 0))
    in_specs = [
        pl.BlockSpec((1, tm, D), lambda b, t: (b, t, 0)),
        pl.BlockSpec((1, D), lambda b, t: (0, 0)),
        pl.BlockSpec((1, 1, D), mod_map),
        pl.BlockSpec((1, 1, D), mod_map),
        pl.BlockSpec(w.shape, lambda b, t: (0, 0)),
    ]
    args = [x, g.reshape(1, D), shift, scale, w]
    if tables is not None:
        in_specs += [pl.BlockSpec((tm, HEAD_DIM), lambda b, t: (t, 0))] * 2
        args += list(tables)
    if qk_gains is not None:
        in_specs += [pl.BlockSpec((1, HEAD_DIM), lambda b, t: (0, 0))] * 2
        args += [qk_gains[0].reshape(1, HEAD_DIM), qk_gains[1].reshape(1, HEAD_DIM)]
    widths = ([(n_pool * HEAD_DIM, F32)] if n_pool else []) + [
        (n_q * HEAD_DIM, BF16), (KV_WIDTH, BF16), (V_WIDTH, BF16)]
    out_specs = [pl.BlockSpec((1, tm, wd), lambda b, t: (b, t, 0)) for wd, _ in widths]
    out_shape = [jax.ShapeDtypeStruct((B, T, wd), dt) for wd, dt in widths]
    return pl.pallas_call(
        functools.partial(_inproj_kernel, n_pool=n_pool, n_q=n_q,
                          qk_norm=qk_gains is not None, rope=tables is not None),
        grid=(B, T // tm),
        in_specs=in_specs, out_specs=out_specs, out_shape=out_shape,
        compiler_params=_cparams("parallel", "parallel"),
        name="inproj",
    )(*args)


def _pool_kernel(u_ref, w_ref, s_ref, o_ref, pad_ref, *, T, chunk):
    halo = 16
    pad_ref[0:halo, :] = jnp.zeros((halo, POOL_WIDTH), F32)
    pad_ref[halo + T:2 * halo + T, :] = jnp.zeros((halo, POOL_WIDTH), F32)
    pad_ref[halo:halo + T, :] = u_ref[0]
    for g, wdw in enumerate(POOL_WINDOWS):
        cols = slice(g * POOL_GROUP, (g + 1) * POOL_GROUP)
        half = wdw // 2
        for r0 in range(0, T, chunk):
            s = pad_ref[halo + r0 - half:halo + r0 - half + chunk, cols]
            for j in range(-half + 1, half):
                s = s + pad_ref[halo + r0 + j:halo + r0 + j + chunk, cols]
            t = r0 + lax.broadcasted_iota(jnp.int32, (chunk, 1), 0)
            cnt = (jnp.minimum(t + half, T) - jnp.maximum(t - half, 0)).astype(F32)
            d = (s / cnt - u_ref[0, r0:r0 + chunk, cols]).astype(BF16)
            y = jnp.dot(d, w_ref[g], preferred_element_type=F32) * s_ref[:, cols]
            o_ref[0, r0:r0 + chunk, cols] = y.astype(BF16)


def _pool(u, pool_w, pool_scale):
    B, T, _ = u.shape
    return pl.pallas_call(
        functools.partial(_pool_kernel, T=T, chunk=min(T, 512)),
        grid=(B,),
        in_specs=[
            pl.BlockSpec((1, T, POOL_WIDTH), lambda b: (b, 0, 0)),
            pl.BlockSpec(pool_w.shape, lambda b: (0, 0, 0)),
            pl.BlockSpec((1, POOL_WIDTH), lambda b: (0, 0)),
        ],
        out_specs=pl.BlockSpec((1, T, POOL_WIDTH), lambda b: (b, 0, 0)),
        out_shape=jax.ShapeDtypeStruct((B, T, POOL_WIDTH), BF16),
        scratch_shapes=[pltpu.VMEM((T + 32, POOL_WIDTH), F32)],
        compiler_params=_cparams("parallel"),
        name="pool",
    )(u, pool_w, pool_scale.reshape(1, POOL_WIDTH))


def _stack_heads(q_ref, h, G):
    return jnp.concatenate(
        [q_ref[0, :, (h * G + g) * HEAD_DIM:(h * G + g + 1) * HEAD_DIM] for g in range(G)], axis=0)


def _qk(q, k):
    return lax.dot_general(q, k, (((1,), (1,)), ((), ())), preferred_element_type=F32)


def _pv(p, v):
    return jnp.dot(p.astype(BF16), v, preferred_element_type=F32)


def _sink_column(sink_ref, h, G, tq):
    return jnp.concatenate([jnp.full((tq, 1), sink_ref[h * G + g] * LOG2E, F32) for g in range(G)], axis=0)


def _store_heads(o_ref, o, h, G, tq):
    for g in range(G):
        o_ref[0, :, (h * G + g) * HEAD_DIM:(h * G + g + 1) * HEAD_DIM] = o[g * tq:(g + 1) * tq].astype(BF16)


def _lane_tile_max(s):
    parts = [s[:, t * HEAD_DIM:(t + 1) * HEAD_DIM] for t in range(s.shape[1] // HEAD_DIM)]
    while len(parts) > 1:
        parts = [jnp.maximum(parts[i], parts[i + 1]) if i + 1 < len(parts) else parts[i]
                 for i in range(0, len(parts), 2)]
    return parts[0]


def _win_attn_kernel(sink_ref, q_ref, k_ref, v_ref, kc_ref, vc_ref, o_ref, s_ref, p_ref, m_ref, *, T, tq):
    G = B_HEADS // B_KV_HEADS
    R = G * tq
    kw_len = tq + 2 * WINDOW
    Tc = kc_ref.shape[1]
    n_keys = kw_len + Tc
    q0 = pl.program_id(1) * tq
    start = pl.multiple_of(jnp.clip(q0 - WINDOW, 0, T - kw_len), WINDOW)
    delta = q0 - start
    r = lax.broadcasted_iota(I32, (R, kw_len), 0) & (tq - 1)
    c = lax.broadcasted_iota(I32, (R, kw_len), 1)
    bias = jnp.where(jnp.abs(r + delta - c) <= WINDOW, 0.0, NEG_INF)

    def sink_tile(h):
        return jnp.concatenate([jnp.full((tq, HEAD_DIM), sink_ref[h * G + g] * LOG2E, F32) for g in range(G)],
                               axis=0)

    def scores(h):
        hs = slice(h * HEAD_DIM, (h + 1) * HEAD_DIM)
        qs = _stack_heads(q_ref, h, G)
        s_loc = _qk(qs, k_ref[0, pl.ds(start, kw_len), hs]) + bias
        s_ctx = _qk(qs, kc_ref[0, :, hs])
        s_ref[h, :, 0:kw_len] = s_loc
        s_ref[h, :, kw_len:n_keys] = s_ctx
        m_t = jnp.maximum(jnp.maximum(_lane_tile_max(s_loc), _lane_tile_max(s_ctx)), sink_tile(h))
        m_ref[h] = jnp.broadcast_to(jnp.max(m_t, axis=-1, keepdims=True), (R, HEAD_DIM))

    def weights(h):
        mb = m_ref[h]
        for t in range(n_keys // HEAD_DIM):
            cols = slice(t * HEAD_DIM, (t + 1) * HEAD_DIM)
            p_ref[h, :, cols] = jnp.exp2(s_ref[h, :, cols] - mb).astype(BF16)

    def values(h):
        vs = slice(h * V_HEAD, (h + 1) * V_HEAD)
        oa = (jnp.dot(p_ref[h, :, 0:kw_len], v_ref[0, pl.ds(start, kw_len), vs], preferred_element_type=F32)
              + jnp.dot(p_ref[h, :, kw_len:n_keys], vc_ref[0, :, vs], preferred_element_type=F32))
        o = oa[:, :HEAD_DIM] / (oa[:, HEAD_DIM:] + jnp.exp2(sink_tile(h) - m_ref[h]))
        _store_heads(o_ref, o, h, G, tq)

    scores(0)
    for h in range(B_KV_HEADS):
        if h + 1 < B_KV_HEADS:
            scores(h + 1)
        weights(h)
        values(h)


def _win_attn(sink, q, k, v, kc, vc, *, tq=WIN_ATTN_TQ):
    B, T, qw = q.shape
    Tc = kc.shape[1]
    return pl.pallas_call(
        functools.partial(_win_attn_kernel, T=T, tq=tq),
        grid=(B, T // tq),
        in_specs=[
            pl.BlockSpec(memory_space=pltpu.SMEM),
            pl.BlockSpec((1, tq, qw), lambda b, t: (b, t, 0)),
            pl.BlockSpec((1, T, KV_WIDTH), lambda b, t: (b, 0, 0)),
            pl.BlockSpec((1, T, V_WIDTH), lambda b, t: (b, 0, 0)),
            pl.BlockSpec((1, Tc, KV_WIDTH), lambda b, t: (b, 0, 0)),
            pl.BlockSpec((1, Tc, V_WIDTH), lambda b, t: (b, 0, 0)),
        ],
        out_specs=pl.BlockSpec((1, tq, qw), lambda b, t: (b, t, 0)),
        out_shape=jax.ShapeDtypeStruct((B, T, qw), BF16),
        scratch_shapes=[pltpu.VMEM((B_KV_HEADS, (B_HEADS // B_KV_HEADS) * tq, tq + 2 * WINDOW + Tc), F32),
                        pltpu.VMEM((B_KV_HEADS, (B_HEADS // B_KV_HEADS) * tq, tq + 2 * WINDOW + Tc), BF16),
                        pltpu.VMEM((B_KV_HEADS, (B_HEADS // B_KV_HEADS) * tq, HEAD_DIM), F32)],
        compiler_params=_cparams("parallel", "arbitrary"),
        name="win_attn",
    )(sink, q, k, v, kc, vc)


def _ctx_attn_kernel(*refs, G, has_sink):
    if has_sink:
        sink_ref, q_ref, k_ref, v_ref, o_ref = refs
    else:
        q_ref, k_ref, v_ref, o_ref = refs
    tq = q_ref.shape[1]
    for h in range(q_ref.shape[2] // (G * HEAD_DIM)):
        qs = _stack_heads(q_ref, h, G)
        s = _qk(qs, k_ref[0, :, h * HEAD_DIM:(h + 1) * HEAD_DIM])
        m = jnp.max(s, axis=-1, keepdims=True)
        if has_sink:
            sink = _sink_column(sink_ref, h, G, tq)
            m = jnp.maximum(m, sink)
        oa = _pv(jnp.exp2(s - m), v_ref[0, :, h * V_HEAD:(h + 1) * V_HEAD])
        den = oa[:, HEAD_DIM:]
        if has_sink:
            den = den + jnp.exp2(sink - m)
        _store_heads(o_ref, oa[:, :HEAD_DIM] / den, h, G, tq)


def _ctx_attn(q, k, v, *, G, sink=None):
    B, Tc, qw = q.shape
    in_specs, args = [], []
    if sink is not None:
        in_specs.append(pl.BlockSpec(memory_space=pltpu.SMEM))
        args.append(sink)
    in_specs += [pl.BlockSpec((1, Tc, a.shape[2]), lambda b: (b, 0, 0)) for a in (q, k, v)]
    return pl.pallas_call(
        functools.partial(_ctx_attn_kernel, G=G, has_sink=sink is not None),
        grid=(B,),
        in_specs=in_specs,
        out_specs=pl.BlockSpec((1, Tc, qw), lambda b: (b, 0, 0)),
        out_shape=jax.ShapeDtypeStruct((B, Tc, qw), BF16),
        compiler_params=_cparams("parallel"),
        name="ctx_attn",
    )(*args, q, k, v)


def _full_attn_kernel(q_ref, kc_ref, vc_ref, k_ref, v_ref, o_ref, s_ref, p_ref, m_ref, *, G, row_block, key_chunk):
    tq = q_ref.shape[1]
    R = G * tq
    half = R // 2
    Tc, Tk = kc_ref.shape[1], s_ref.shape[2]
    n_lane_tiles = Tk // HEAD_DIM
    n_heads = q_ref.shape[2] // (G * HEAD_DIM)

    def scores(h):
        hs = slice(h * HEAD_DIM, (h + 1) * HEAD_DIM)
        qs = _stack_heads(q_ref, h, G)
        for r0 in (0, half):
            q_half = qs[r0:r0 + half]
            s = _qk(q_half, kc_ref[0, :, hs])
            s_ref[h % 2, r0:r0 + half, 0:Tc] = s
            m_run = _lane_tile_max(s)
            for c0 in range(0, Tk - Tc, key_chunk):
                s = _qk(q_half, k_ref[0, c0:c0 + key_chunk, hs])
                s_ref[h % 2, r0:r0 + half, Tc + c0:Tc + c0 + key_chunk] = s
                m_run = jnp.maximum(m_run, _lane_tile_max(s))
            m_ref[h % 2, r0:r0 + half, :] = jnp.broadcast_to(
                jnp.max(m_run, axis=-1, keepdims=True), (half, HEAD_DIM))

    def weights(h):
        for r0 in range(0, R, row_block):
            mb = m_ref[h % 2, r0:r0 + row_block, :]
            for t in range(n_lane_tiles):
                cols = slice(t * HEAD_DIM, (t + 1) * HEAD_DIM)
                p_ref[h % 2, r0:r0 + row_block, cols] = jnp.exp2(
                    s_ref[h % 2, r0:r0 + row_block, cols] - mb).astype(BF16)

    def values(h):
        vs = slice(h * V_HEAD, (h + 1) * V_HEAD)
        for r0 in (0, half):
            oa = (jnp.dot(p_ref[h % 2, r0:r0 + half, 0:Tc], vc_ref[0, :, vs], preferred_element_type=F32)
                  + jnp.dot(p_ref[h % 2, r0:r0 + half, Tc:Tk], v_ref[0, :, vs], preferred_element_type=F32))
            o = oa[:, :HEAD_DIM] / oa[:, HEAD_DIM:]
            for g in range(G):
                lo, hi = g * tq, (g + 1) * tq
                a, bnd = max(lo, r0), min(hi, r0 + half)
                if a < bnd:
                    o_ref[0, a - lo:bnd - lo, (h * G + g) * HEAD_DIM:(h * G + g + 1) * HEAD_DIM] = (
                        o[a - r0:bnd - r0].astype(BF16))

    scores(0)
    for h in range(n_heads):
        if h + 1 < n_heads:
            scores(h + 1)
        weights(h)
        values(h)


def _full_attn(q, kc, vc, k, v, *, G, tq):
    B, T, qw = q.shape
    Tc = kc.shape[1]
    Tk = Tc + T
    return pl.pallas_call(
        functools.partial(_full_attn_kernel, G=G, row_block=32, key_chunk=512),
        grid=(B, T // tq),
        in_specs=[
            pl.BlockSpec((1, tq, qw), lambda b, t: (b, t, 0)),
            pl.BlockSpec((1, Tc, KV_WIDTH), lambda b, t: (b, 0, 0), pipeline_mode=pl.Buffered(1)),
            pl.BlockSpec((1, Tc, V_WIDTH), lambda b, t: (b, 0, 0), pipeline_mode=pl.Buffered(1)),
            pl.BlockSpec((1, T, KV_WIDTH), lambda b, t: (b, 0, 0), pipeline_mode=pl.Buffered(1)),
            pl.BlockSpec((1, T, V_WIDTH), lambda b, t: (b, 0, 0), pipeline_mode=pl.Buffered(1)),
        ],
        out_specs=pl.BlockSpec((1, tq, qw), lambda b, t: (b, t, 0)),
        out_shape=jax.ShapeDtypeStruct((B, T, qw), BF16),
        scratch_shapes=[pltpu.VMEM((2, G * tq, Tk), F32), pltpu.VMEM((2, G * tq, Tk), BF16),
                        pltpu.VMEM((2, G * tq, HEAD_DIM), F32)],
        compiler_params=_cparams("parallel", "arbitrary"),
        name="full_attn",
    )(q, kc, vc, k, v)


def _outproj_kernel(*refs, n_y):
    y_refs = refs[:n_y]
    (w_ref, x_ref, g1_ref, ng_ref, sh_ref, sc_ref, rw_ref, xo_ref, h_ref, lg_ref) = refs[n_y:]
    acc = None
    row = 0
    for y_ref in y_refs:
        wd = y_ref.shape[2]
        part = jnp.dot(y_ref[0], w_ref[row:row + wd, :], preferred_element_type=F32)
        acc = part if acc is None else acc + part
        row += wd
    x = x_ref[0] + g1_ref[0] * acc
    xo_ref[0] = x
    h = _norm_mod(x, ng_ref[...], sh_ref[0], sc_ref[0])
    h_ref[0] = _pack_rows(h)
    E = N_EXPERTS
    h_hi = h.astype(BF16)
    h_lo = (h - h_hi.astype(F32)).astype(BF16)
    dot = functools.partial(jnp.dot, preferred_element_type=F32)
    p_hi = dot(h_hi, rw_ref[...])
    lg_ref[0] = p_hi[:, :E] + (p_hi[:, E:] + dot(h_lo, rw_ref[:, :E]))


def _outproj(ys, w, x, g1, ng, shift, scale, rw3, *, tm):
    B, T, D = x.shape
    per_batch = g1.shape[0] == B
    mod_map = (lambda b, t: (b, 0, 0)) if per_batch else (lambda b, t: (0, 0, 0))
    tile = lambda wd: pl.BlockSpec((1, tm, wd), lambda b, t: (b, t, 0))
    in_specs = [tile(y.shape[2]) for y in ys] + [
        pl.BlockSpec(w.shape, lambda b, t: (0, 0)),
        tile(D),
        pl.BlockSpec((1, 1, D), mod_map),
        pl.BlockSpec((1, D), lambda b, t: (0, 0)),
        pl.BlockSpec((1, 1, D), mod_map),
        pl.BlockSpec((1, 1, D), mod_map),
        pl.BlockSpec(rw3.shape, lambda b, t: (0, 0)),
    ]
    return pl.pallas_call(
        functools.partial(_outproj_kernel, n_y=len(ys)),
        grid=(B, T // tm),
        in_specs=in_specs,
        out_specs=[tile(D), tile(D // 2), tile(N_EXPERTS)],
        out_shape=[jax.ShapeDtypeStruct((B, T, D), F32),
                   jax.ShapeDtypeStruct((B, T, D // 2), U32),
                   jax.ShapeDtypeStruct((B, T, N_EXPERTS), F32)],
        compiler_params=_cparams("parallel", "parallel"),
        name="outproj",
    )(*ys, w, x, g1, ng.reshape(1, D), shift, scale, rw3)


def _token_prefix(x, tri):
    T = x.shape[0]
    xb = x.astype(BF16)
    local = [jnp.dot(tri, xb[j:j + LANES], preferred_element_type=F32) for j in range(0, T, LANES)]
    out, run = [], None
    for blk in local:
        blk = blk if run is None else blk + run
        run = blk[LANES - 1:LANES, :]
        out.append(blk)
    return jnp.concatenate(out, axis=0)


def _route_kernel(lg_ref, idx_ref, gate_ref, sel_ref, slot_ref, lo_ref, aff_ref, pos_ref, *, T, cap, tile):
    E = N_EXPERTS
    lg = lg_ref[0]
    ex = jnp.exp(lg - jnp.max(lg, axis=-1, keepdims=True))
    aff = ex / jnp.sum(ex, axis=-1, keepdims=True)
    bits = pltpu.bitcast(aff, I32)

    def bisect(_, c):
        lo, hi = c
        mid = lo + ((hi - lo) >> 1)
        cnt = jnp.sum(jnp.where(bits >= mid, 1.0, 0.0), axis=0, keepdims=True)
        ge = cnt >= cap
        return jnp.where(ge, mid, lo), jnp.where(ge, hi, mid)

    thr, _ = lax.fori_loop(0, 31, bisect, (jnp.zeros((1, E), I32), jnp.full((1, E), 0x3F800001, I32)))
    gt = bits > thr
    eq = bits == thr
    r = lax.broadcasted_iota(I32, (LANES, LANES), 0)
    cidx = lax.broadcasted_iota(I32, (LANES, LANES), 1)
    tri = jnp.where(cidx <= r, 1.0, 0.0).astype(BF16)
    eq_f = jnp.where(eq, 1.0, 0.0)
    need = cap - jnp.sum(jnp.where(gt, 1.0, 0.0), axis=0, keepdims=True)
    tie_rank = _token_prefix(eq_f, tri) - eq_f
    sel = gt | (eq & (tie_rank < need))
    sel_f = jnp.where(sel, 1.0, 0.0)
    pos = _token_prefix(sel_f, tri)
    sel_ref[0] = sel_f
    slot_ref[0] = pos - 1.0
    lo_ref[0, 0:1, :] = jnp.zeros((1, E), I32)
    for j in range(1, T // tile + 1):
        lo_ref[0, j:j + 1, :] = pos[j * tile - 1:j * tile, :].astype(I32)
    aff_ref[...] = aff
    pos_ref[...] = jnp.where(sel, pos, 0.0)

    chunk = min(T, 256)
    n_lt = (cap + LANES - 1) // LANES
    for e in range(E):
        def body(ci, accs):
            rows = pl.ds(pl.multiple_of(ci * chunk, chunk), chunk)
            p_col = jnp.broadcast_to(pos_ref[rows, e:e + 1], (chunk, LANES))
            a_col = jnp.broadcast_to(aff_ref[rows, e:e + 1], (chunk, LANES))
            tok = (ci * chunk + lax.broadcasted_iota(I32, (chunk, LANES), 0)).astype(F32)
            out = []
            for lt in range(n_lt):
                want = (lax.broadcasted_iota(I32, (chunk, LANES), 1) + (lt * LANES + 1)).astype(F32)
                hit = p_col == want
                ti = jnp.where(hit, tok, 0.0).reshape(chunk // 8, 8, LANES).sum(axis=0)
                gi = jnp.where(hit, a_col, 0.0).reshape(chunk // 8, 8, LANES).sum(axis=0)
                out += [accs[2 * lt] + ti, accs[2 * lt + 1] + gi]
            return tuple(out)

        accs = lax.fori_loop(0, T // chunk, body, tuple(jnp.zeros((8, LANES), F32) for _ in range(2 * n_lt)))
        for lt in range(n_lt):
            w = min(LANES, cap - lt * LANES)
            idx_ref[0, e:e + 1, lt * LANES:lt * LANES + w] = (
                jnp.sum(accs[2 * lt], axis=0, keepdims=True)[:, :w].astype(I32))
            gate_ref[0, e:e + 1, lt * LANES:lt * LANES + w] = jnp.sum(accs[2 * lt + 1], axis=0, keepdims=True)[:, :w]


def _route(logits, *, cap, tile):
    B, T, E = logits.shape
    nb = T // tile + 1
    tok_spec = pl.BlockSpec((1, T, E), lambda b: (b, 0, 0))
    slot_spec = pl.BlockSpec((1, E, cap), lambda b: (b, 0, 0))
    return pl.pallas_call(
        functools.partial(_route_kernel, T=T, cap=cap, tile=tile),
        grid=(B,),
        in_specs=[tok_spec],
        out_specs=[slot_spec, slot_spec, tok_spec, tok_spec, pl.BlockSpec((1, nb, E), lambda b: (b, 0, 0))],
        out_shape=[jax.ShapeDtypeStruct((B, E, cap), I32), jax.ShapeDtypeStruct((B, E, cap), F32),
                   jax.ShapeDtypeStruct((B, T, E), F32), jax.ShapeDtypeStruct((B, T, E), F32),
                   jax.ShapeDtypeStruct((B, nb, E), I32)],
        scratch_shapes=[pltpu.VMEM((T, E), F32), pltpu.VMEM((T, E), F32)],
        compiler_params=_cparams("arbitrary"),
        name="route",
    )(logits)


def _expert_kernel(idx_ref, h_hbm, gate_ref, wg_ref, wu_ref, wd_ref, o_ref, rows_ref, xs_ref, acc_ref, sem,
                   *, tm, n_tiles):
    n, f = pl.program_id(0), pl.program_id(1)
    nf = pl.num_programs(1)
    per = tm // nf
    slot = lax.rem(n, 2)

    def row_copy(tile, r, buf):
        return pltpu.make_async_copy(h_hbm.at[pl.ds(idx_ref[tile * tm + r], 1), :],
                                     rows_ref.at[buf, pl.ds(r, 1), :], sem.at[buf])

    def tile_wait(buf):
        pltpu.make_async_copy(h_hbm.at[pl.ds(0, tm), :], rows_ref.at[buf], sem.at[buf]).wait()

    @pl.when((n == 0) & (f == 0))
    def _():
        for r in range(tm):
            row_copy(0, r, 0).start()

    @pl.when(f == 0)
    def _():
        tile_wait(slot)
        xs_ref[...] = _unpack_rows(rows_ref[slot])
        acc_ref[...] = jnp.zeros(acc_ref.shape, F32)

    nxt = jnp.minimum(n + 1, n_tiles - 1)
    for i in range(per):
        row_copy(nxt, f * per + i, 1 - slot).start()

    xs = xs_ref[...]
    a = jnp.dot(xs, wg_ref[0, 0].astype(BF16), preferred_element_type=F32)
    u = jnp.dot(xs, wu_ref[0, 0].astype(BF16), preferred_element_type=F32)
    act = (a * (1.0 / (1.0 + jnp.exp(-a))) * u).astype(BF16)
    acc_ref[...] += jnp.dot(act, wd_ref[0, 0].astype(BF16), preferred_element_type=F32)

    @pl.when(f == nf - 1)
    def _():
        blk = min(tm, LANES)
        eye = lax.broadcasted_iota(I32, (blk, blk), 0) == lax.broadcasted_iota(I32, (blk, blk), 1)
        for r0 in range(0, tm, blk):
            g_row = jnp.broadcast_to(gate_ref[0, :, r0:r0 + blk], (blk, blk))
            g_col = jnp.sum(jnp.where(eye, g_row, 0.0), axis=-1, keepdims=True)
            o_ref[r0:r0 + blk, :] = (acc_ref[r0:r0 + blk, :] * g_col).astype(BF16)

    @pl.when((n == n_tiles - 1) & (f == nf - 1))
    def _():
        tile_wait(1 - slot)


def _experts(idx, h, gate, wg, wu, wd, layer, *, tm, tf=EXPERT_FF):
    N, half = h.shape
    D = 2 * half
    _, E, _, F = wg.shape
    n_tiles = idx.shape[0] // tm
    per_e = n_tiles // E
    nf = F // tf
    assert tm % nf == 0
    return pl.pallas_call(
        functools.partial(_expert_kernel, tm=tm, n_tiles=n_tiles),
        grid_spec=pltpu.PrefetchScalarGridSpec(
            num_scalar_prefetch=1,
            grid=(n_tiles, nf),
            in_specs=[
                pl.BlockSpec(memory_space=pl.ANY),
                pl.BlockSpec((1, 1, tm), lambda n, f, idx: (n, 0, 0)),
                pl.BlockSpec((1, 1, D, tf), lambda n, f, idx: (layer, n // per_e, 0, f)),
                pl.BlockSpec((1, 1, D, tf), lambda n, f, idx: (layer, n // per_e, 0, f)),
                pl.BlockSpec((1, 1, tf, D), lambda n, f, idx: (layer, n // per_e, f, 0)),
            ],
            out_specs=pl.BlockSpec((tm, D), lambda n, f, idx: (n, 0)),
            scratch_shapes=[pltpu.VMEM((2, tm, half), U32), pltpu.VMEM((tm, D), BF16), pltpu.VMEM((tm, D), F32),
                            pltpu.SemaphoreType.DMA((2,))]),
        out_shape=jax.ShapeDtypeStruct((n_tiles * tm, D), BF16),
        compiler_params=_cparams("arbitrary", "arbitrary"),
        name="experts",
    )(idx, h, gate, wg, wu, wd)


def _combine_kernel(lo_ref, x_ref, g_ref, sel_ref, slot_ref, ye_hbm, *rest, cap, win, n_tiles, seg_stride, final):
    if final:
        fg_ref, o_ref, stage_ref, extra_ref, hot_ref, sem = rest
    else:
        o_ref, stage_ref, extra_ref, hot_ref, sem = rest
    E = N_EXPERTS
    b, t = pl.program_id(0), pl.program_id(1)
    step = b * n_tiles + t
    n_steps = pl.num_programs(0) * n_tiles
    buf = lax.rem(step, 2)
    tm = x_ref.shape[1]

    def seg_lo(bb, tt, e):
        return lo_ref[(bb * (n_tiles + 1) + tt) * E + e]

    def win_start(bb, tt, e):
        w0 = jnp.minimum((seg_lo(bb, tt, e) // 16) * 16, cap - win)
        return pl.multiple_of((e * seg_stride + bb) * cap + w0, 16), w0

    def window_copy(bb, tt, e, k):
        row, _ = win_start(bb, tt, e)
        return pltpu.make_async_copy(ye_hbm.at[pl.ds(row, win), :], stage_ref.at[k, pl.ds(e * win, win), :],
                                     sem.at[k])

    @pl.when(step == 0)
    def _():
        for e in range(E):
            window_copy(b, t, e, 0).start()

    @pl.when(step + 1 < n_steps)
    def _():
        nb = jnp.where(t + 1 < n_tiles, b, b + 1)
        nt = jnp.where(t + 1 < n_tiles, t + 1, 0)
        for e in range(E):
            window_copy(nb, nt, e, 1 - buf).start()

    for e in range(E):
        window_copy(b, t, e, buf).wait()

    sel = sel_ref[0] > 0.5
    slot = slot_ref[0]
    lane = lax.broadcasted_iota(I32, (tm, win), 1).astype(F32)
    for e in range(E):
        _, w0 = win_start(b, t, e)
        rel = jnp.where(sel[:, e:e + 1], slot[:, e:e + 1] - w0.astype(F32), -1.0)
        hot_ref[:, e * win:(e + 1) * win] = jnp.where(
            jnp.broadcast_to(rel, (tm, win)) == lane, 1.0, 0.0).astype(BF16)
    D = x_ref.shape[2]
    cw = min(D, 512)
    for c0 in range(0, D, cw):
        cols = slice(c0, c0 + cw)
        moe = jnp.dot(hot_ref[...], stage_ref[buf, :, cols], preferred_element_type=F32)
        o_ref[0, :, cols] = x_ref[0, :, cols] + g_ref[0, :, cols] * moe

    if win < cap:
        for e in range(E):
            _, w0 = win_start(b, t, e)
            hi = seg_lo(b, t + 1, e)
            n_extra = jnp.maximum(hi - (w0 + win) + win - 1, 0) // win

            def extra(k, carry, e=e, w0=w0):
                c0 = w0 + k * win
                ck = jnp.minimum(c0, cap - win)
                row = pl.multiple_of((e * seg_stride + b) * cap + ck, 16)
                cp = pltpu.make_async_copy(ye_hbm.at[pl.ds(row, win), :], extra_ref, sem.at[2])
                cp.start()
                cp.wait()
                s_e = slot[:, e:e + 1]
                ok = sel[:, e:e + 1] & (s_e >= c0.astype(F32)) & (s_e < (c0 + win).astype(F32))
                rel = jnp.where(ok, s_e - ck.astype(F32), -1.0)
                h1 = jnp.where(jnp.broadcast_to(rel, (tm, win)) == lane, 1.0, 0.0).astype(BF16)
                o_ref[0] += g_ref[0] * jnp.dot(h1, extra_ref[...], preferred_element_type=F32)
                return carry

            lax.fori_loop(1, n_extra + 1, extra, 0)

    if final:
        rb = min(tm, 64)

        def norm_rows(i, carry):
            rows = pl.ds(pl.multiple_of(i * rb, rb), rb)
            y = o_ref[0, rows, :]
            o_ref[0, rows, :] = y * lax.rsqrt(jnp.mean(y * y, axis=-1, keepdims=True) + EPS) * fg_ref[...]
            return carry

        lax.fori_loop(0, tm // rb, norm_rows, 0)


def _combine(lo, x, g2, sel, slot, ye, *, cap, tile, seg_stride, final_g=None):
    B, T, D = x.shape
    n_tiles = T // tile
    win = min(cap, LANES)
    per_batch = g2.shape[0] == B
    mod_map = (lambda b, t, lo: (b, 0, 0)) if per_batch else (lambda b, t, lo: (0, 0, 0))
    tok = lambda w: pl.BlockSpec((1, tile, w), lambda b, t, lo: (b, t, 0))
    in_specs = [tok(D), pl.BlockSpec((1, 1, D), mod_map), tok(N_EXPERTS), tok(N_EXPERTS),
                pl.BlockSpec(memory_space=pl.ANY)]
    args = [x, g2, sel, slot, ye]
    if final_g is not None:
        in_specs.append(pl.BlockSpec((1, D), lambda b, t, lo: (0, 0)))
        args.append(final_g.reshape(1, D))
    return pl.pallas_call(
        functools.partial(_combine_kernel, cap=cap, win=win, n_tiles=n_tiles, seg_stride=seg_stride,
                          final=final_g is not None),
        grid_spec=pltpu.PrefetchScalarGridSpec(
            num_scalar_prefetch=1,
            grid=(B, n_tiles),
            in_specs=in_specs,
            out_specs=tok(D),
            scratch_shapes=[pltpu.VMEM((2, N_EXPERTS * win, D), BF16), pltpu.VMEM((win, D), BF16),
                            pltpu.VMEM((tile, N_EXPERTS * win), BF16), pltpu.SemaphoreType.DMA((3,))]),
        out_shape=jax.ShapeDtypeStruct((B, T, D), F32),
        compiler_params=_cparams("arbitrary", "arbitrary"),
        name="combine",
    )(lo.reshape(-1), *args)


def _rope_tables(n_tokens):
    rows = n_tokens // GRID_W
    row = jnp.broadcast_to(jnp.arange(rows)[:, None], (rows, GRID_W)).reshape(-1).astype(F32)
    col = jnp.broadcast_to(jnp.arange(GRID_W)[None, :], (rows, GRID_W)).reshape(-1).astype(F32)
    inv = ROPE_THETA ** (-jnp.arange(0, ROT_AXIS, 2, dtype=F32) / ROT_AXIS)
    ang_r, ang_c = row[:, None] * inv, col[:, None] * inv
    cos = jnp.concatenate([jnp.cos(ang_r)] * 2 + [jnp.cos(ang_c)] * 2, axis=1)
    sin = jnp.concatenate([-jnp.sin(ang_r), jnp.sin(ang_r), -jnp.sin(ang_c), jnp.sin(ang_c)], axis=1)
    return cos, sin


def _moe(h, logits, x, g2, wg, wu, wd, layer, *, tile, final_g=None):
    B, T, D = x.shape
    cap = CAPACITY_FACTOR * T // N_EXPERTS
    idx, gate, sel, slot, lo = _route(logits, cap=cap, tile=tile)
    rows = jnp.swapaxes(idx + (jnp.arange(B, dtype=I32) * T)[:, None, None], 0, 1).reshape(-1)
    tm = min(B * cap, EXPERT_ROWS)
    gate = jnp.swapaxes(gate, 0, 1).reshape(-1, 1, tm)
    ye = _experts(rows, h.reshape(B * T, D // 2), gate, wg, wu, wd, layer, tm=tm)
    return _combine(lo, x, g2, sel, slot, ye, cap=cap, tile=tile, seg_stride=B, final_g=final_g)


def kernel(x, c, ctx, c_ctx, mod_w, mod_b, norm_g, w_out, router_w, expert_w_gate, expert_w_up,
           expert_w_down, ab_w_in, pool_w, pool_scale, ab_sink, c_w_in, c_q_norm_g, c_k_norm_g,
           final_norm_g):
    B, T, D = x.shape
    Tc = ctx.shape[1]
    tables = _rope_tables(T)

    cond = jnp.zeros((COND_ROWS, D), F32).at[:B].set(c).at[B].set(c_ctx)
    mod = _adaln(cond, mod_w, mod_b)

    for i in range(DEPTH):
        last = i == DEPTH - 1
        j = i // 2
        mx = [mod[i, :B, k * D:(k + 1) * D].reshape(B, 1, D) for k in range(6)]
        mc = [mod[i, B:B + 1, k * D:(k + 1) * D].reshape(1, 1, D) for k in range(6)]
        rw_hi = router_w[i].astype(BF16)
        rw3 = jnp.concatenate([rw_hi, (router_w[i] - rw_hi.astype(F32)).astype(BF16)], axis=1)
        wo = w_out[i].astype(BF16)
        wg, wu, wd = expert_w_gate, expert_w_up, expert_w_down
        if i % 2 == 0:
            w_in = ab_w_in[j].astype(BF16)
            u, q, k, v = _inproj(x, norm_g[i, 0], mx[0], mx[1], w_in, n_pool=4, n_q=B_HEADS,
                                 tm=TOKEN_TILE, tables=tables)
            uc, qc, kc, vc = _inproj(ctx, norm_g[i, 0], mc[0], mc[1], w_in, n_pool=4, n_q=B_HEADS, tm=Tc)
            pw = pool_w[j].astype(BF16)
            ys = [_pool(u, pw, pool_scale[j]), _win_attn(ab_sink[j], q, k, v, kc, vc)]
            if not last:
                ycs = [_pool(uc, pw, pool_scale[j]),
                       _ctx_attn(qc, kc, vc, G=B_HEADS // B_KV_HEADS, sink=ab_sink[j])]
        else:
            w_in = c_w_in[j].astype(BF16)
            gains = (c_q_norm_g[j], c_k_norm_g[j])
            q, k, v = _inproj(x, norm_g[i, 0], mx[0], mx[1], w_in, n_pool=0, n_q=C_HEADS,
                              tm=TOKEN_TILE, tables=tables, qk_gains=gains)
            qc, kc, vc = _inproj(ctx, norm_g[i, 0], mc[0], mc[1], w_in, n_pool=0, n_q=C_HEADS,
                                 tm=Tc, qk_gains=gains)
            ys = [_full_attn(q, kc, vc, k, v, G=C_HEADS // C_KV_HEADS, tq=FULL_ATTN_TQ)]
            if not last:
                ycs = [_ctx_attn(qc, kc, vc, G=C_HEADS // C_KV_HEADS)]

        x, h2, lg = _outproj(ys, wo, x, mx[2], norm_g[i, 1], mx[3], mx[4], rw3, tm=TOKEN_TILE)
        x = _moe(h2, lg, x, mx[5], wg, wu, wd, i, tile=TOKEN_TILE, final_g=final_norm_g if last else None)
        if not last:
            ctx, h2c, lgc = _outproj(ycs, wo, ctx, mc[2], norm_g[i, 1], mc[3], mc[4], rw3, tm=Tc)
            ctx = _moe(h2c, lgc, ctx, mc[5], wg, wu, wd, i, tile=Tc)

    return x
```

```python
import functools

import jax
import jax.numpy as jnp
from jax import lax
from jax.experimental import pallas as pl
from jax.experimental.pallas import tpu as pltpu

F32 = jnp.float32
BF16 = jnp.bfloat16
I32 = jnp.int32
U32 = jnp.uint32
LANES = 128

D_MODEL = 2048
DEPTH = 2
GRID_W = 64
HEAD_DIM = 128
EPS = 1e-6
NEG_INF = -1e30
ATTN_SCALE = HEAD_DIM ** -0.5
POOL_WINDOWS = (2, 4, 8, 16)
POOL_GROUP = 128
POOL_WIDTH = 512
WINDOW = 128
B_HEADS = 12
B_KV_HEADS = 4
C_HEADS = 16
C_KV_HEADS = 4
KV_WIDTH = 4 * HEAD_DIM
V_HEAD = 2 * HEAD_DIM
V_WIDTH = 4 * V_HEAD
LOG2E = 1.4426950408889634
Q_SCALE = ATTN_SCALE * LOG2E
ROT_AXIS = HEAD_DIM // 2
ROPE_THETA = 10000.0
N_EXPERTS = 16
CAPACITY_FACTOR = 2
COND_ROWS = 16
VMEM_LIMIT = 56 * 1024 * 1024
TOKEN_TILE = 512
ADALN_COLS = 1024
FULL_ATTN_TQ = 128
WIN_ATTN_TQ = 256
EXPERT_ROWS = 1024
EXPERT_FF = 256


def _cparams(*sem):
    return pltpu.CompilerParams(dimension_semantics=sem, vmem_limit_bytes=VMEM_LIMIT)


def _adaln_kernel(cond_ref, w_ref, b_ref, o_ref):
    cnd = cond_ref[...]
    s = cnd * (1.0 / (1.0 + jnp.exp(-cnd)))
    o_ref[0] = jnp.dot(s.astype(BF16), w_ref[0].astype(BF16), preferred_element_type=F32) + b_ref[0]


def _adaln(cond, mod_w, mod_b):
    tn = ADALN_COLS
    six_d = mod_w.shape[-1]
    return pl.pallas_call(
        _adaln_kernel,
        grid=(DEPTH, six_d // tn),
        in_specs=[
            pl.BlockSpec((COND_ROWS, D_MODEL), lambda i, j: (0, 0)),
            pl.BlockSpec((1, D_MODEL, tn), lambda i, j: (i, 0, j)),
            pl.BlockSpec((1, 1, tn), lambda i, j: (i, 0, j)),
        ],
        out_specs=pl.BlockSpec((1, COND_ROWS, tn), lambda i, j: (i, 0, j)),
        out_shape=jax.ShapeDtypeStruct((DEPTH, COND_ROWS, six_d), F32),
        compiler_params=_cparams("parallel", "parallel"),
        name="adaln",
    )(cond, mod_w, mod_b.reshape(DEPTH, 1, six_d))


def _norm_mod(x, g, shift, scale):
    ms = jnp.mean(x * x, axis=-1, keepdims=True)
    h = x * lax.rsqrt(ms + EPS) * g
    return h * (1.0 + scale) + shift


def _rope(a, cos, sin_signed):
    lane = lax.broadcasted_iota(jnp.int32, a.shape, 1)
    first = (lane & (ROT_AXIS - 1)) < (ROT_AXIS // 2)
    partner = jnp.where(first, pltpu.roll(a, HEAD_DIM - ROT_AXIS // 2, axis=1),
                        pltpu.roll(a, ROT_AXIS // 2, axis=1))
    return a * cos + partner * sin_signed


def _pack_rows(h):
    half = h.shape[1] // 2
    lo = pltpu.bitcast(h[:, :half].astype(BF16).astype(F32), U32)
    hi = pltpu.bitcast(h[:, half:].astype(BF16).astype(F32), U32)
    return (lo >> 16) | (hi & jnp.uint32(0xFFFF0000))


def _unpack_rows(w):
    lo = pltpu.bitcast(w << 16, F32).astype(BF16)
    hi = pltpu.bitcast(w & jnp.uint32(0xFFFF0000), F32).astype(BF16)
    return jnp.concatenate([lo, hi], axis=1)


def _inproj_kernel(*refs, n_pool, n_q, qk_norm, rope):
    x_ref, g_ref, sh_ref, sc_ref, w_ref = refs[:5]
    pos = 5
    if rope:
        cos_ref, sin_ref = refs[pos:pos + 2]
        pos += 2
    if qk_norm:
        qg_ref, kg_ref = refs[pos:pos + 2]
        pos += 2
    outs = refs[pos:]
    if n_pool:
        u_ref, q_ref, k_ref, v_ref = outs
    else:
        q_ref, k_ref, v_ref = outs

    hb = _norm_mod(x_ref[0], g_ref[...], sh_ref[0], sc_ref[0]).astype(BF16)
    n_heads = w_ref.shape[1] // HEAD_DIM
    group = 4
    for c0 in range(0, n_heads, group):
        acc = jnp.dot(hb, w_ref[:, c0 * HEAD_DIM:(c0 + group) * HEAD_DIM], preferred_element_type=F32)
        for j in range(group):
            c = c0 + j
            a = acc[:, j * HEAD_DIM:(j + 1) * HEAD_DIM]
            if c < n_pool:
                u_ref[0, :, c * HEAD_DIM:(c + 1) * HEAD_DIM] = a
                continue
            c -= n_pool
            is_q = c < n_q
            is_k = (not is_q) and c < n_q + 4
            if is_q or is_k:
                if qk_norm:
                    gain = qg_ref[...] if is_q else kg_ref[...]
                    a = a * lax.rsqrt(jnp.mean(a * a, axis=-1, keepdims=True) + EPS) * gain
                if rope:
                    a = _rope(a, cos_ref[...], sin_ref[...])
            if is_q:
                q_ref[0, :, c * HEAD_DIM:(c + 1) * HEAD_DIM] = (a * Q_SCALE).astype(BF16)
            elif is_k:
                c -= n_q
                k_ref[0, :, c * HEAD_DIM:(c + 1) * HEAD_DIM] = a.astype(BF16)
            else:
                c -= n_q + 4
                v_ref[0, :, c * V_HEAD:c * V_HEAD + HEAD_DIM] = a.astype(BF16)
                v_ref[0, :, c * V_HEAD + HEAD_DIM:(c + 1) * V_HEAD] = jnp.ones(a.shape, BF16)


def _inproj(x, g, shift, scale, w, *, n_pool, n_q, tm, tables=None, qk_gains=None):
    B, T, D = x.shape
    per_batch = shift.shape[0] == B
    mod_map = (lambda b, t: (b, 0, 0)) if per_batch else (lambda b, t: (0, 0, 0))
    in_specs = [
        pl.BlockSpec((1, tm, D), lambda b, t: (b, t, 0)),
        pl.BlockSpec((1, D), lambda b, t: (0, 0)),
        pl.BlockSpec((1, 1, D), mod_map),
        pl.BlockSpec((1, 1, D), mod_map),
        pl.BlockSpec(w.shape, lambda b, t: (0, 0)),
    ]
    args = [x, g.reshape(1, D), shift, scale, w]
    if tables is not None:
        in_specs += [pl.BlockSpec((tm, HEAD_DIM), lambda b, t: (t, 0))] * 2
        args += list(tables)
    if qk_gains is not None:
        in_specs += [pl.BlockSpec((1, HEAD_DIM), lambda b, t: (0, 0))] * 2
        args += [qk_gains[0].reshape(1, HEAD_DIM), qk_gains[1].reshape(1, HEAD_DIM)]
    widths = ([(n_pool * HEAD_DIM, F32)] if n_pool else []) + [
        (n_q * HEAD_DIM, BF16), (KV_WIDTH, BF16), (V_WIDTH, BF16)]
    out_specs = [pl.BlockSpec((1, tm, wd), lambda b, t: (b, t, 0)) for wd, _ in widths]
    out_shape = [jax.ShapeDtypeStruct((B, T, wd), dt) for wd, dt in widths]
    return pl.pallas_call(
        functools.partial(_inproj_kernel, n_pool=n_pool, n_q=n_q,
                          qk_norm=qk_gains is not None, rope=tables is not None),
        grid=(B, T // tm),
        in_specs=in_specs, out_specs=out_specs, out_shape=out_shape,
        compiler_params=_cparams("parallel", "parallel"),
        name="inproj",
    )(*args)


def _pool_kernel(u_ref, w_ref, s_ref, o_ref, pad_ref, *, T, chunk):
    halo = 16
    pad_ref[0:halo, :] = jnp.zeros((halo, POOL_WIDTH), F32)
    pad_ref[halo + T:2 * halo + T, :] = jnp.zeros((halo, POOL_WIDTH), F32)
    pad_ref[halo:halo + T, :] = u_ref[0]
    for g, wdw in enumerate(POOL_WINDOWS):
        cols = slice(g * POOL_GROUP, (g + 1) * POOL_GROUP)
        half = wdw // 2
        for r0 in range(0, T, chunk):
            s = pad_ref[halo + r0 - half:halo + r0 - half + chunk, cols]
            for j in range(-half + 1, half):
                s = s + pad_ref[halo + r0 + j:halo + r0 + j + chunk, cols]
            t = r0 + lax.broadcasted_iota(jnp.int32, (chunk, 1), 0)
            cnt = (jnp.minimum(t + half, T) - jnp.maximum(t - half, 0)).astype(F32)
            d = (s / cnt - u_ref[0, r0:r0 + chunk, cols]).astype(BF16)
            y = jnp.dot(d, w_ref[g], preferred_element_type=F32) * s_ref[:, cols]
            o_ref[0, r0:r0 + chunk, cols] = y.astype(BF16)


def _pool(u, pool_w, pool_scale):
    B, T, _ = u.shape
    return pl.pallas_call(
        functools.partial(_pool_kernel, T=T, chunk=min(T, 512)),
        grid=(B,),
        in_specs=[
            pl.BlockSpec((1, T, POOL_WIDTH), lambda b: (b, 0, 0)),
            pl.BlockSpec(pool_w.shape, lambda b: (0, 0, 0)),
            pl.BlockSpec((1, POOL_WIDTH), lambda b: (0, 0)),
        ],
        out_specs=pl.BlockSpec((1, T, POOL_WIDTH), lambda b: (b, 0, 0)),
        out_shape=jax.ShapeDtypeStruct((B, T, POOL_WIDTH), BF16),
        scratch_shapes=[pltpu.VMEM((T + 32, POOL_WIDTH), F32)],
        compiler_params=_cparams("parallel"),
        name="pool",
    )(u, pool_w, pool_scale.reshape(1, POOL_WIDTH))


def _stack_heads(q_ref, h, G):
    return jnp.concatenate(
        [q_ref[0, :, (h * G + g) * HEAD_DIM:(h * G + g + 1) * HEAD_DIM] for g in range(G)], axis=0)


def _qk(q, k):
    return lax.dot_general(q, k, (((1,), (1,)), ((), ())), preferred_element_type=F32)


def _pv(p, v):
    return jnp.dot(p.astype(BF16), v, preferred_element_type=F32)


def _sink_column(sink_ref, h, G, tq):
    return jnp.concatenate([jnp.full((tq, 1), sink_ref[h * G + g] * LOG2E, F32) for g in range(G)], axis=0)


def _store_heads(o_ref, o, h, G, tq):
    for g in range(G):
        o_ref[0, :, (h * G + g) * HEAD_DIM:(h * G + g + 1) * HEAD_DIM] = o[g * tq:(g + 1) * tq].astype(BF16)


def _lane_tile_max(s):
    parts = [s[:, t * HEAD_DIM:(t + 1) * HEAD_DIM] for t in range(s.shape[1] // HEAD_DIM)]
    while len(parts) > 1:
        parts = [jnp.maximum(parts[i], parts[i + 1]) if i + 1 < len(parts) else parts[i]
                 for i in range(0, len(parts), 2)]
    return parts[0]


def _win_attn_kernel(sink_ref, q_ref, k_ref, v_ref, kc_ref, vc_ref, o_ref, s_ref, p_ref, m_ref, *, T, tq):
    G = B_HEADS // B_KV_HEADS
    R = G * tq
    kw_len = tq + 2 * WINDOW
    Tc = kc_ref.shape[1]
    n_keys = kw_len + Tc
    q0 = pl.program_id(1) * tq
    start = pl.multiple_of(jnp.clip(q0 - WINDOW, 0, T - kw_len), WINDOW)
    delta = q0 - start
    r = lax.broadcasted_iota(I32, (R, kw_len), 0) & (tq - 1)
    c = lax.broadcasted_iota(I32, (R, kw_len), 1)
    bias = jnp.where(jnp.abs(r + delta - c) <= WINDOW, 0.0, NEG_INF)

    def sink_tile(h):
        return jnp.concatenate([jnp.full((tq, HEAD_DIM), sink_ref[h * G + g] * LOG2E, F32) for g in range(G)],
                               axis=0)

    def scores(h):
        hs = slice(h * HEAD_DIM, (h + 1) * HEAD_DIM)
        qs = _stack_heads(q_ref, h, G)
        s_loc = _qk(qs, k_ref[0, pl.ds(start, kw_len), hs]) + bias
        s_ctx = _qk(qs, kc_ref[0, :, hs])
        s_ref[h, :, 0:kw_len] = s_loc
        s_ref[h, :, kw_len:n_keys] = s_ctx
        m_t = jnp.maximum(jnp.maximum(_lane_tile_max(s_loc), _lane_tile_max(s_ctx)), sink_tile(h))
        m_ref[h] = jnp.broadcast_to(jnp.max(m_t, axis=-1, keepdims=True), (R, HEAD_DIM))

    def weights(h):
        mb = m_ref[h]
        for t in range(n_keys // HEAD_DIM):
            cols = slice(t * HEAD_DIM, (t + 1) * HEAD_DIM)
            p_ref[h, :, cols] = jnp.exp2(s_ref[h, :, cols] - mb).astype(BF16)

    def values(h):
        vs = slice(h * V_HEAD, (h + 1) * V_HEAD)
        oa = (jnp.dot(p_ref[h, :, 0:kw_len], v_ref[0, pl.ds(start, kw_len), vs], preferred_element_type=F32)
              + jnp.dot(p_ref[h, :, kw_len:n_keys], vc_ref[0, :, vs], preferred_element_type=F32))
        o = oa[:, :HEAD_DIM] / (oa[:, HEAD_DIM:] + jnp.exp2(sink_tile(h) - m_ref[h]))
        _store_heads(o_ref, o, h, G, tq)

    scores(0)
    for h in range(B_KV_HEADS):
        if h + 1 < B_KV_HEADS:
            scores(h + 1)
        weights(h)
        values(h)


def _win_attn(sink, q, k, v, kc, vc, *, tq=WIN_ATTN_TQ):
    B, T, qw = q.shape
    Tc = kc.shape[1]
    return pl.pallas_call(
        functools.partial(_win_attn_kernel, T=T, tq=tq),
        grid=(B, T // tq),
        in_specs=[
            pl.BlockSpec(memory_space=pltpu.SMEM),
            pl.BlockSpec((1, tq, qw), lambda b, t: (b, t, 0)),
            pl.BlockSpec((1, T, KV_WIDTH), lambda b, t: (b, 0, 0)),
            pl.BlockSpec((1, T, V_WIDTH), lambda b, t: (b, 0, 0)),
            pl.BlockSpec((1, Tc, KV_WIDTH), lambda b, t: (b, 0, 0)),
            pl.BlockSpec((1, Tc, V_WIDTH), lambda b, t: (b, 0, 0)),
        ],
        out_specs=pl.BlockSpec((1, tq, qw), lambda b, t: (b, t, 0)),
        out_shape=jax.ShapeDtypeStruct((B, T, qw), BF16),
        scratch_shapes=[pltpu.VMEM((B_KV_HEADS, (B_HEADS // B_KV_HEADS) * tq, tq + 2 * WINDOW + Tc), F32),
                        pltpu.VMEM((B_KV_HEADS, (B_HEADS // B_KV_HEADS) * tq, tq + 2 * WINDOW + Tc), BF16),
                        pltpu.VMEM((B_KV_HEADS, (B_HEADS // B_KV_HEADS) * tq, HEAD_DIM), F32)],
        compiler_params=_cparams("parallel", "arbitrary"),
        name="win_attn",
    )(sink, q, k, v, kc, vc)


def _ctx_attn_kernel(*refs, G, has_sink):
    if has_sink:
        sink_ref, q_ref, k_ref, v_ref, o_ref = refs
    else:
        q_ref, k_ref, v_ref, o_ref = refs
    tq = q_ref.shape[1]
    for h in range(q_ref.shape[2] // (G * HEAD_DIM)):
        qs = _stack_heads(q_ref, h, G)
        s = _qk(qs, k_ref[0, :, h * HEAD_DIM:(h + 1) * HEAD_DIM])
        m = jnp.max(s, axis=-1, keepdims=True)
        if has_sink:
            sink = _sink_column(sink_ref, h, G, tq)
            m = jnp.maximum(m, sink)
        oa = _pv(jnp.exp2(s - m), v_ref[0, :, h * V_HEAD:(h + 1) * V_HEAD])
        den = oa[:, HEAD_DIM:]
        if has_sink:
            den = den + jnp.exp2(sink - m)
        _store_heads(o_ref, oa[:, :HEAD_DIM] / den, h, G, tq)


def _ctx_attn(q, k, v, *, G, sink=None):
    B, Tc, qw = q.shape
    in_specs, args = [], []
    if sink is not None:
        in_specs.append(pl.BlockSpec(memory_space=pltpu.SMEM))
        args.append(sink)
    in_specs += [pl.BlockSpec((1, Tc, a.shape[2]), lambda b: (b, 0, 0)) for a in (q, k, v)]
    return pl.pallas_call(
        functools.partial(_ctx_attn_kernel, G=G, has_sink=sink is not None),
        grid=(B,),
        in_specs=in_specs,
        out_specs=pl.BlockSpec((1, Tc, qw), lambda b: (b, 0, 0)),
        out_shape=jax.ShapeDtypeStruct((B, Tc, qw), BF16),
        compiler_params=_cparams("parallel"),
        name="ctx_attn",
    )(*args, q, k, v)


def _full_attn_kernel(q_ref, kc_ref, vc_ref, k_ref, v_ref, o_ref, s_ref, p_ref, m_ref, *, G, row_block, key_chunk):
    tq = q_ref.shape[1]
    R = G * tq
    half = R // 2
    Tc, Tk = kc_ref.shape[1], s_ref.shape[2]
    n_lane_tiles = Tk // HEAD_DIM
    n_heads = q_ref.shape[2] // (G * HEAD_DIM)

    def scores(h):
        hs = slice(h * HEAD_DIM, (h + 1) * HEAD_DIM)
        qs = _stack_heads(q_ref, h, G)
        for r0 in (0, half):
            q_half = qs[r0:r0 + half]
            s = _qk(q_half, kc_ref[0, :, hs])
            s_ref[h % 2, r0:r0 + half, 0:Tc] = s
            m_run = _lane_tile_max(s)
            for c0 in range(0, Tk - Tc, key_chunk):
                s = _qk(q_half, k_ref[0, c0:c0 + key_chunk, hs])
                s_ref[h % 2, r0:r0 + half, Tc + c0:Tc + c0 + key_chunk] = s
                m_run = jnp.maximum(m_run, _lane_tile_max(s))
            m_ref[h % 2, r0:r0 + half, :] = jnp.broadcast_to(
                jnp.max(m_run, axis=-1, keepdims=True), (half, HEAD_DIM))

    def weights(h):
        for r0 in range(0, R, row_block):
            mb = m_ref[h % 2, r0:r0 + row_block, :]
            for t in range(n_lane_tiles):
                cols = slice(t * HEAD_DIM, (t + 1) * HEAD_DIM)
                p_ref[h % 2, r0:r0 + row_block, cols] = jnp.exp2(
                    s_ref[h % 2, r0:r0 + row_block, cols] - mb).astype(BF16)

    def values(h):
        vs = slice(h * V_HEAD, (h + 1) * V_HEAD)
        for r0 in (0, half):
            oa = (jnp.dot(p_ref[h % 2, r0:r0 + half, 0:Tc], vc_ref[0, :, vs], preferred_element_type=F32)
                  + jnp.dot(p_ref[h % 2, r0:r0 + half, Tc:Tk], v_ref[0, :, vs], preferred_element_type=F32))
            o = oa[:, :HEAD_DIM] / oa[:, HEAD_DIM:]
            for g in range(G):
                lo, hi = g * tq, (g + 1) * tq
                a, bnd = max(lo, r0), min(hi, r0 + half)
                if a < bnd:
                    o_ref[0, a - lo:bnd - lo, (h * G + g) * HEAD_DIM:(h * G + g + 1) * HEAD_DIM] = (
                        o[a - r0:bnd - r0].astype(BF16))

    scores(0)
    for h in range(n_heads):
        if h + 1 < n_heads:
            scores(h + 1)
        weights(h)
        values(h)


def _full_attn(q, kc, vc, k, v, *, G, tq):
    B, T, qw = q.shape
    Tc = kc.shape[1]
    Tk = Tc + T
    return pl.pallas_call(
        functools.partial(_full_attn_kernel, G=G, row_block=32, key_chunk=512),
        grid=(B, T // tq),
        in_specs=[
            pl.BlockSpec((1, tq, qw), lambda b, t: (b, t, 0)),
            pl.BlockSpec((1, Tc, KV_WIDTH), lambda b, t: (b, 0, 0), pipeline_mode=pl.Buffered(1)),
            pl.BlockSpec((1, Tc, V_WIDTH), lambda b, t: (b, 0, 0), pipeline_mode=pl.Buffered(1)),
            pl.BlockSpec((1, T, KV_WIDTH), lambda b, t: (b, 0, 0), pipeline_mode=pl.Buffered(1)),
            pl.BlockSpec((1, T, V_WIDTH), lambda b, t: (b, 0, 0), pipeline_mode=pl.Buffered(1)),
        ],
        out_specs=pl.BlockSpec((1, tq, qw), lambda b, t: (b, t, 0)),
        out_shape=jax.ShapeDtypeStruct((B, T, qw), BF16),
        scratch_shapes=[pltpu.VMEM((2, G * tq, Tk), F32), pltpu.VMEM((2, G * tq, Tk), BF16),
                        pltpu.VMEM((2, G * tq, HEAD_DIM), F32)],
        compiler_params=_cparams("parallel", "arbitrary"),
        name="full_attn",
    )(q, kc, vc, k, v)


def _outproj_kernel(*refs, n_y):
    y_refs = refs[:n_y]
    (w_ref, x_ref, g1_ref, ng_ref, sh_ref, sc_ref, rw_ref, xo_ref, h_ref, lg_ref) = refs[n_y:]
    acc = None
    row = 0
    for y_ref in y_refs:
        wd = y_ref.shape[2]
        part = jnp.dot(y_ref[0], w_ref[row:row + wd, :], preferred_element_type=F32)
        acc = part if acc is None else acc + part
        row += wd
    x = x_ref[0] + g1_ref[0] * acc
    xo_ref[0] = x
    h = _norm_mod(x, ng_ref[...], sh_ref[0], sc_ref[0])
    h_ref[0] = _pack_rows(h)
    E = N_EXPERTS
    h_hi = h.astype(BF16)
    h_lo = (h - h_hi.astype(F32)).astype(BF16)
    dot = functools.partial(jnp.dot, preferred_element_type=F32)
    p_hi = dot(h_hi, rw_ref[...])
    lg_ref[0] = p_hi[:, :E] + (p_hi[:, E:] + dot(h_lo, rw_ref[:, :E]))


def _outproj(ys, w, x, g1, ng, shift, scale, rw3, *, tm):
    B, T, D = x.shape
    per_batch = g1.shape[0] == B
    mod_map = (lambda b, t: (b, 0, 0)) if per_batch else (lambda b, t: (0, 0, 0))
    tile = lambda wd: pl.BlockSpec((1, tm, wd), lambda b, t: (b, t, 0))
    in_specs = [tile(y.shape[2]) for y in ys] + [
        pl.BlockSpec(w.shape, lambda b, t: (0, 0)),
        tile(D),
        pl.BlockSpec((1, 1, D), mod_map),
        pl.BlockSpec((1, D), lambda b, t: (0, 0)),
        pl.BlockSpec((1, 1, D), mod_map),
        pl.BlockSpec((1, 1, D), mod_map),
        pl.BlockSpec(rw3.shape, lambda b, t: (0, 0)),
    ]
    return pl.pallas_call(
        functools.partial(_outproj_kernel, n_y=len(ys)),
        grid=(B, T // tm),
        in_specs=in_specs,
        out_specs=[tile(D), tile(D // 2), tile(N_EXPERTS)],
        out_shape=[jax.ShapeDtypeStruct((B, T, D), F32),
                   jax.ShapeDtypeStruct((B, T, D // 2), U32),
                   jax.ShapeDtypeStruct((B, T, N_EXPERTS), F32)],
        compiler_params=_cparams("parallel", "parallel"),
        name="outproj",
    )(*ys, w, x, g1, ng.reshape(1, D), shift, scale, rw3)


def _token_prefix(x, tri):
    T = x.shape[0]
    xb = x.astype(BF16)
    local = [jnp.dot(tri, xb[j:j + LANES], preferred_element_type=F32) for j in range(0, T, LANES)]
    out, run = [], None
    for blk in local:
        blk = blk if run is None else blk + run
        run = blk[LANES - 1:LANES, :]
        out.append(blk)
    return jnp.concatenate(out, axis=0)


def _split_exact(a, n):
    parts, rest = [], a
    for _ in range(n):
        p = rest.astype(BF16)
        parts.append(p)
        rest = rest - p.astype(F32)
    return parts


def _to_lanes(x, eye):
    nt = lambda p: lax.dot_general(eye, p, (((1,), (1,)), ((), ())), preferred_element_type=F32)
    parts = _split_exact(x, 3)
    return (nt(parts[0]) + nt(parts[1])) + nt(parts[2])


def _route_kernel(lg_ref, idx_ref, gate_ref, sel_ref, slot_ref, lo_ref, *scratch, T, cap, tile, hier):
    E = N_EXPERTS
    lg = lg_ref[0]
    ex = jnp.exp(lg - jnp.max(lg, axis=-1, keepdims=True))
    aff = ex / jnp.sum(ex, axis=-1, keepdims=True)
    bits = pltpu.bitcast(aff, I32)

    groups = LANES // E if T >= 8 * LANES else 1
    bits_p = jnp.concatenate([bits[k * (T // groups):(k + 1) * (T // groups), :] for k in range(groups)], axis=1)

    def bisect(_, c):
        lo, hi = c
        mid = lo + ((hi - lo) >> 1)
        cnt = jnp.sum(jnp.where(bits_p >= mid, 1.0, 0.0), axis=0, keepdims=True)
        shift = groups * E // 2
        while shift >= E:
            cnt = cnt + pltpu.roll(cnt, shift, 1)
            shift //= 2
        ge = cnt >= cap
        return jnp.where(ge, mid, lo), jnp.where(ge, hi, mid)

    thr, _ = lax.fori_loop(0, 31, bisect, (jnp.zeros((1, groups * E), I32),
                                           jnp.full((1, groups * E), 0x3F800001, I32)))
    thr = thr[:, :E]
    gt = bits > thr
    eq = bits == thr
    r = lax.broadcasted_iota(I32, (LANES, LANES), 0)
    cidx = lax.broadcasted_iota(I32, (LANES, LANES), 1)
    tri = jnp.where(cidx <= r, 1.0, 0.0).astype(BF16)
    eq_f = jnp.where(eq, 1.0, 0.0)
    need = cap - jnp.sum(jnp.where(gt, 1.0, 0.0), axis=0, keepdims=True)
    tie_rank = _token_prefix(eq_f, tri) - eq_f
    sel = gt | (eq & (tie_rank < need))
    sel_f = jnp.where(sel, 1.0, 0.0)
    pos = _token_prefix(sel_f, tri)
    sel_ref[0] = sel_f
    slot_ref[0] = pos - 1.0
    lo_ref[0, 0:1, :] = jnp.zeros((1, E), I32)
    for j in range(1, T // tile + 1):
        lo_ref[0, j:j + 1, :] = pos[j * tile - 1:j * tile, :].astype(I32)

    if hier:
        cnt_ref, aff_ref = scratch
        nblk = T // LANES
        eye = jnp.where(lax.broadcasted_iota(I32, (E, E), 0) == lax.broadcasted_iota(I32, (E, E), 1),
                        1.0, 0.0).astype(BF16)
        for j in range(nblk):
            cnt_ref[j] = _to_lanes(pos[j * LANES:(j + 1) * LANES, :], eye)
            aff_ref[j] = _to_lanes(aff[j * LANES:(j + 1) * LANES, :], eye)
        ends = jnp.concatenate([pos[(j + 1) * LANES - 1:(j + 1) * LANES, :] for j in range(nblk)], axis=0)
        starts = jnp.concatenate([jnp.zeros((1, E), F32), ends[:nblk - 1]], axis=0)
        ends_t, starts_t = _to_lanes(ends, eye), _to_lanes(starts, eye)
        s_col = lax.broadcasted_iota(I32, (cap, nblk), 0).astype(F32)
        blk_id = lax.broadcasted_iota(I32, (cap, nblk), 1).astype(F32)
        s_lane = lax.broadcasted_iota(I32, (cap, LANES), 0).astype(F32)
        lane = lax.broadcasted_iota(I32, (cap, LANES), 1).astype(F32)
        for e in range(E):
            in_blk = (starts_t[e:e + 1, :] <= s_col) & (s_col < ends_t[e:e + 1, :])
            hot = jnp.where(in_blk, 1.0, 0.0).astype(BF16)
            base = jnp.sum(jnp.where(in_blk, blk_id, 0.0), axis=-1, keepdims=True) * float(LANES)
            c_hi, c_lo = _split_exact(cnt_ref[:, e, :], 2)
            counts = (jnp.dot(hot, c_hi, preferred_element_type=F32)
                      + jnp.dot(hot, c_lo, preferred_element_type=F32))
            local = jnp.sum(jnp.where(counts <= s_lane, 1.0, 0.0), axis=-1, keepdims=True)
            a_parts = _split_exact(aff_ref[:, e, :], 3)
            affs = ((jnp.dot(hot, a_parts[0], preferred_element_type=F32)
                     + jnp.dot(hot, a_parts[1], preferred_element_type=F32))
                    + jnp.dot(hot, a_parts[2], preferred_element_type=F32))
            idx_ref[0, e] = (base + local).astype(I32)
            gate_ref[0, e] = jnp.sum(jnp.where(lane == local, affs, 0.0), axis=-1, keepdims=True)
    else:
        aff_ref, pos_ref = scratch
        aff_ref[...] = aff
        pos_ref[...] = jnp.where(sel, pos, 0.0)
        for e in range(E):
            p_col = jnp.broadcast_to(pos_ref[:, e:e + 1], (T, cap))
            a_col = jnp.broadcast_to(aff_ref[:, e:e + 1], (T, cap))
            tok = lax.broadcasted_iota(I32, (T, cap), 0).astype(F32)
            hit = p_col == (lax.broadcasted_iota(I32, (T, cap), 1) + 1).astype(F32)
            eye_c = lax.broadcasted_iota(I32, (cap, cap), 0) == lax.broadcasted_iota(I32, (cap, cap), 1)
            t_row = jnp.sum(jnp.where(hit, tok, 0.0), axis=0, keepdims=True)
            g_row = jnp.sum(jnp.where(hit, a_col, 0.0), axis=0, keepdims=True)
            idx_ref[0, e] = jnp.sum(jnp.where(eye_c, jnp.broadcast_to(t_row, (cap, cap)), 0.0), axis=-1,
                                    keepdims=True).astype(I32)
            gate_ref[0, e] = jnp.sum(jnp.where(eye_c, jnp.broadcast_to(g_row, (cap, cap)), 0.0), axis=-1,
                                     keepdims=True)


def _route(logits, *, cap, tile):
    B, T, E = logits.shape
    nb = T // tile + 1
    hier = T % LANES == 0 and T // LANES >= 16
    tok_spec = pl.BlockSpec((1, T, E), lambda b: (b, 0, 0))
    slot_spec = pl.BlockSpec((1, E, cap, 1), lambda b: (b, 0, 0, 0))
    scratch = ([pltpu.VMEM((T // LANES, E, LANES), F32)] * 2 if hier else [pltpu.VMEM((T, E), F32)] * 2)
    return pl.pallas_call(
        functools.partial(_route_kernel, T=T, cap=cap, tile=tile, hier=hier),
        grid=(B,),
        in_specs=[tok_spec],
        out_specs=[slot_spec, slot_spec, tok_spec, tok_spec, pl.BlockSpec((1, nb, E), lambda b: (b, 0, 0))],
        out_shape=[jax.ShapeDtypeStruct((B, E, cap, 1), I32), jax.ShapeDtypeStruct((B, E, cap, 1), F32),
                   jax.ShapeDtypeStruct((B, T, E), F32), jax.ShapeDtypeStruct((B, T, E), F32),
                   jax.ShapeDtypeStruct((B, nb, E), I32)],
        scratch_shapes=scratch,
        compiler_params=_cparams("arbitrary"),
        name="route",
    )(logits)


def _expert_kernel(idx_ref, h_hbm, gate_ref, wg_ref, wu_ref, wd_ref, o_ref, rows_ref, xs_ref, acc_ref, sem,
                   *, tm, n_tiles):
    n, f = pl.program_id(0), pl.program_id(1)
    nf = pl.num_programs(1)
    per = tm // nf
    slot = lax.rem(n, 2)

    def row_copy(tile, r, buf):
        return pltpu.make_async_copy(h_hbm.at[pl.ds(idx_ref[tile * tm + r], 1), :],
                                     rows_ref.at[buf, pl.ds(r, 1), :], sem.at[buf])

    def tile_wait(buf):
        pltpu.make_async_copy(h_hbm.at[pl.ds(0, tm), :], rows_ref.at[buf], sem.at[buf]).wait()

    @pl.when((n == 0) & (f == 0))
    def _():
        for r in range(tm):
            row_copy(0, r, 0).start()

    @pl.when(f == 0)
    def _():
        tile_wait(slot)
        xs_ref[...] = _unpack_rows(rows_ref[slot])
        acc_ref[...] = jnp.zeros(acc_ref.shape, F32)

    nxt = jnp.minimum(n + 1, n_tiles - 1)
    for i in range(per):
        row_copy(nxt, f * per + i, 1 - slot).start()

    xs = xs_ref[...]
    a = jnp.dot(xs, wg_ref[0, 0].astype(BF16), preferred_element_type=F32)
    u = jnp.dot(xs, wu_ref[0, 0].astype(BF16), preferred_element_type=F32)
    act = (a * (1.0 / (1.0 + jnp.exp(-a))) * u).astype(BF16)
    acc_ref[...] += jnp.dot(act, wd_ref[0, 0].astype(BF16), preferred_element_type=F32)

    @pl.when(f == nf - 1)
    def _():
        blk = min(tm, LANES)
        eye = lax.broadcasted_iota(I32, (blk, blk), 0) == lax.broadcasted_iota(I32, (blk, blk), 1)
        for r0 in range(0, tm, blk):
            g_row = jnp.broadcast_to(gate_ref[0, :, r0:r0 + blk], (blk, blk))
            g_col = jnp.sum(jnp.where(eye, g_row, 0.0), axis=-1, keepdims=True)
            o_ref[r0:r0 + blk, :] = (acc_ref[r0:r0 + blk, :] * g_col).astype(BF16)

    @pl.when((n == n_tiles - 1) & (f == nf - 1))
    def _():
        tile_wait(1 - slot)


def _experts(idx, h, gate, wg, wu, wd, layer, *, tm, tf=EXPERT_FF):
    N, half = h.shape
    D = 2 * half
    _, E, _, F = wg.shape
    n_tiles = idx.shape[0] // tm
    per_e = n_tiles // E
    nf = F // tf
    assert tm % nf == 0
    return pl.pallas_call(
        functools.partial(_expert_kernel, tm=tm, n_tiles=n_tiles),
        grid_spec=pltpu.PrefetchScalarGridSpec(
            num_scalar_prefetch=1,
            grid=(n_tiles, nf),
            in_specs=[
                pl.BlockSpec(memory_space=pl.ANY),
                pl.BlockSpec((1, 1, tm), lambda n, f, idx: (n, 0, 0)),
                pl.BlockSpec((1, 1, D, tf), lambda n, f, idx: (layer, n // per_e, 0, f)),
                pl.BlockSpec((1, 1, D, tf), lambda n, f, idx: (layer, n // per_e, 0, f)),
                pl.BlockSpec((1, 1, tf, D), lambda n, f, idx: (layer, n // per_e, f, 0)),
            ],
            out_specs=pl.BlockSpec((tm, D), lambda n, f, idx: (n, 0)),
            scratch_shapes=[pltpu.VMEM((2, tm, half), U32), pltpu.VMEM((tm, D), BF16), pltpu.VMEM((tm, D), F32),
                            pltpu.SemaphoreType.DMA((2,))]),
        out_shape=jax.ShapeDtypeStruct((n_tiles * tm, D), BF16),
        compiler_params=_cparams("arbitrary", "arbitrary"),
        name="experts",
    )(idx, h, gate, wg, wu, wd)


def _combine_kernel(lo_ref, x_ref, g_ref, sel_ref, slot_ref, ye_hbm, *rest, cap, win, n_tiles, seg_stride, final):
    if final:
        fg_ref, o_ref, stage_ref, extra_ref, hot_ref, sem = rest
    else:
        o_ref, stage_ref, extra_ref, hot_ref, sem = rest
    E = N_EXPERTS
    b, t = pl.program_id(0), pl.program_id(1)
    step = b * n_tiles + t
    n_steps = pl.num_programs(0) * n_tiles
    buf = lax.rem(step, 2)
    tm = x_ref.shape[1]

    def seg_lo(bb, tt, e):
        return lo_ref[(bb * (n_tiles + 1) + tt) * E + e]

    def win_start(bb, tt, e):
        w0 = jnp.minimum((seg_lo(bb, tt, e) // 16) * 16, cap - win)
        return pl.multiple_of((e * seg_stride + bb) * cap + w0, 16), w0

    def window_copy(bb, tt, e, k):
        row, _ = win_start(bb, tt, e)
        return pltpu.make_async_copy(ye_hbm.at[pl.ds(row, win), :], stage_ref.at[k, pl.ds(e * win, win), :],
                                     sem.at[k])

    @pl.when(step == 0)
    def _():
        for e in range(E):
            window_copy(b, t, e, 0).start()

    @pl.when(step + 1 < n_steps)
    def _():
        nb = jnp.where(t + 1 < n_tiles, b, b + 1)
        nt = jnp.where(t + 1 < n_tiles, t + 1, 0)
        for e in range(E):
            window_copy(nb, nt, e, 1 - buf).start()

    for e in range(E):
        window_copy(b, t, e, buf).wait()

    sel = sel_ref[0] > 0.5
    slot = slot_ref[0]
    lane = lax.broadcasted_iota(I32, (tm, win), 1).astype(F32)
    for e in range(E):
        _, w0 = win_start(b, t, e)
        rel = jnp.where(sel[:, e:e + 1], slot[:, e:e + 1] - w0.astype(F32), -1.0)
        hot_ref[:, e * win:(e + 1) * win] = jnp.where(
            jnp.broadcast_to(rel, (tm, win)) == lane, 1.0, 0.0).astype(BF16)
    D = x_ref.shape[2]
    cw = min(D, 512)
    for c0 in range(0, D, cw):
        cols = slice(c0, c0 + cw)
        moe = jnp.dot(hot_ref[...], stage_ref[buf, :, cols], preferred_element_type=F32)
        o_ref[0, :, cols] = x_ref[0, :, cols] + g_ref[0, :, cols] * moe

    if win < cap:
        for e in range(E):
            _, w0 = win_start(b, t, e)
            hi = seg_lo(b, t + 1, e)
            n_extra = jnp.maximum(hi - (w0 + win) + win - 1, 0) // win

            def extra(k, carry, e=e, w0=w0):
                c0 = w0 + k * win
                ck = jnp.minimum(c0, cap - win)
                row = pl.multiple_of((e * seg_stride + b) * cap + ck, 16)
                cp = pltpu.make_async_copy(ye_hbm.at[pl.ds(row, win), :], extra_ref, sem.at[2])
                cp.start()
                cp.wait()
                s_e = slot[:, e:e + 1]
                ok = sel[:, e:e + 1] & (s_e >= c0.astype(F32)) & (s_e < (c0 + win).astype(F32))
                rel = jnp.where(ok, s_e - ck.astype(F32), -1.0)
                h1 = jnp.where(jnp.broadcast_to(rel, (tm, win)) == lane, 1.0, 0.0).astype(BF16)
                o_ref[0] += g_ref[0] * jnp.dot(h1, extra_ref[...], preferred_element_type=F32)
                return carry

            lax.fori_loop(1, n_extra + 1, extra, 0)

    if final:
        rb = min(tm, 64)

        def norm_rows(i, carry):
            rows = pl.ds(pl.multiple_of(i * rb, rb), rb)
            y = o_ref[0, rows, :]
            o_ref[0, rows, :] = y * lax.rsqrt(jnp.mean(y * y, axis=-1, keepdims=True) + EPS) * fg_ref[...]
            return carry

        lax.fori_loop(0, tm // rb, norm_rows, 0)


def _combine(lo, x, g2, sel, slot, ye, *, cap, tile, seg_stride, final_g=None):
    B, T, D = x.shape
    n_tiles = T // tile
    win = min(cap, LANES)
    per_batch = g2.shape[0] == B
    mod_map = (lambda b, t, lo: (b, 0, 0)) if per_batch else (lambda b, t, lo: (0, 0, 0))
    tok = lambda w: pl.BlockSpec((1, tile, w), lambda b, t, lo: (b, t, 0))
    in_specs = [tok(D), pl.BlockSpec((1, 1, D), mod_map), tok(N_EXPERTS), tok(N_EXPERTS),
                pl.BlockSpec(memory_space=pl.ANY)]
    args = [x, g2, sel, slot, ye]
    if final_g is not None:
        in_specs.append(pl.BlockSpec((1, D), lambda b, t, lo: (0, 0)))
        args.append(final_g.reshape(1, D))
    return pl.pallas_call(
        functools.partial(_combine_kernel, cap=cap, win=win, n_tiles=n_tiles, seg_stride=seg_stride,
                          final=final_g is not None),
        grid_spec=pltpu.PrefetchScalarGridSpec(
            num_scalar_prefetch=1,
            grid=(B, n_tiles),
            in_specs=in_specs,
            out_specs=tok(D),
            scratch_shapes=[pltpu.VMEM((2, N_EXPERTS * win, D), BF16), pltpu.VMEM((win, D), BF16),
                            pltpu.VMEM((tile, N_EXPERTS * win), BF16), pltpu.SemaphoreType.DMA((3,))]),
        out_shape=jax.ShapeDtypeStruct((B, T, D), F32),
        compiler_params=_cparams("arbitrary", "arbitrary"),
        name="combine",
    )(lo.reshape(-1), *args)


def _rope_tables(n_tokens):
    rows = n_tokens // GRID_W
    row = jnp.broadcast_to(jnp.arange(rows)[:, None], (rows, GRID_W)).reshape(-1).astype(F32)
    col = jnp.broadcast_to(jnp.arange(GRID_W)[None, :], (rows, GRID_W)).reshape(-1).astype(F32)
    inv = ROPE_THETA ** (-jnp.arange(0, ROT_AXIS, 2, dtype=F32) / ROT_AXIS)
    ang_r, ang_c = row[:, None] * inv, col[:, None] * inv
    cos = jnp.concatenate([jnp.cos(ang_r)] * 2 + [jnp.cos(ang_c)] * 2, axis=1)
    sin = jnp.concatenate([-jnp.sin(ang_r), jnp.sin(ang_r), -jnp.sin(ang_c), jnp.sin(ang_c)], axis=1)
    return cos, sin


def _moe(h, logits, x, g2, wg, wu, wd, layer, *, tile, final_g=None):
    B, T, D = x.shape
    cap = CAPACITY_FACTOR * T // N_EXPERTS
    idx, gate, sel, slot, lo = _route(logits, cap=cap, tile=tile)
    rows = jnp.swapaxes(idx[..., 0] + (jnp.arange(B, dtype=I32) * T)[:, None, None], 0, 1).reshape(-1)
    tm = min(B * cap, EXPERT_ROWS)
    gate = jnp.swapaxes(gate[..., 0], 0, 1).reshape(-1, 1, tm)
    ye = _experts(rows, h.reshape(B * T, D // 2), gate, wg, wu, wd, layer, tm=tm)
    return _combine(lo, x, g2, sel, slot, ye, cap=cap, tile=tile, seg_stride=B, final_g=final_g)


def kernel(x, c, ctx, c_ctx, mod_w, mod_b, norm_g, w_out, router_w, expert_w_gate, expert_w_up,
           expert_w_down, ab_w_in, pool_w, pool_scale, ab_sink, c_w_in, c_q_norm_g, c_k_norm_g,
           final_norm_g):
    B, T, D = x.shape
    Tc = ctx.shape[1]
    tables = _rope_tables(T)

    cond = jnp.zeros((COND_ROWS, D), F32).at[:B].set(c).at[B].set(c_ctx)
    mod = _adaln(cond, mod_w, mod_b)

    for i in range(DEPTH):
        last = i == DEPTH - 1
        j = i // 2
        mx = [mod[i, :B, k * D:(k + 1) * D].reshape(B, 1, D) for k in range(6)]
        mc = [mod[i, B:B + 1, k * D:(k + 1) * D].reshape(1, 1, D) for k in range(6)]
        rw_hi = router_w[i].astype(BF16)
        rw3 = jnp.concatenate([rw_hi, (router_w[i] - rw_hi.astype(F32)).astype(BF16)], axis=1)
        wo = w_out[i].astype(BF16)
        wg, wu, wd = expert_w_gate, expert_w_up, expert_w_down
        if i % 2 == 0:
            w_in = ab_w_in[j].astype(BF16)
            u, q, k, v = _inproj(x, norm_g[i, 0], mx[0], mx[1], w_in, n_pool=4, n_q=B_HEADS,
                                 tm=TOKEN_TILE, tables=tables)
            uc, qc, kc, vc = _inproj(ctx, norm_g[i, 0], mc[0], mc[1], w_in, n_pool=4, n_q=B_HEADS, tm=Tc)
            pw = pool_w[j].astype(BF16)
            ys = [_pool(u, pw, pool_scale[j]), _win_attn(ab_sink[j], q, k, v, kc, vc)]
            if not last:
                ycs = [_pool(uc, pw, pool_scale[j]),
                       _ctx_attn(qc, kc, vc, G=B_HEADS // B_KV_HEADS, sink=ab_sink[j])]
        else:
            w_in = c_w_in[j].astype(BF16)
            gains = (c_q_norm_g[j], c_k_norm_g[j])
            q, k, v = _inproj(x, norm_g[i, 0], mx[0], mx[1], w_in, n_pool=0, n_q=C_HEADS,
                              tm=TOKEN_TILE, tables=tables, qk_gains=gains)
            qc, kc, vc = _inproj(ctx, norm_g[i, 0], mc[0], mc[1], w_in, n_pool=0, n_q=C_HEADS,
                                 tm=Tc, qk_gains=gains)
            ys = [_full_attn(q, kc, vc, k, v, G=C_HEADS // C_KV_HEADS, tq=FULL_ATTN_TQ)]
            if not last:
                ycs = [_ctx_attn(qc, kc, vc, G=C_HEADS // C_KV_HEADS)]

        x, h2, lg = _outproj(ys, wo, x, mx[2], norm_g[i, 1], mx[3], mx[4], rw3, tm=TOKEN_TILE)
        x = _moe(h2, lg, x, mx[5], wg, wu, wd, i, tile=TOKEN_TILE, final_g=final_norm_g if last else None)
        if not last:
            ctx, h2c, lgc = _outproj(ycs, wo, ctx, mc[2], norm_g[i, 1], mc[3], mc[4], rw3, tm=Tc)
            ctx = _moe(h2c, lgc, ctx, mc[5], wg, wu, wd, i, tile=Tc)

    return x
```

```python
import functools

import jax
import jax.numpy as jnp
from jax import lax
from jax.experimental import pallas as pl
from jax.experimental.pallas import tpu as pltpu

F32 = jnp.float32
BF16 = jnp.bfloat16
I32 = jnp.int32
U32 = jnp.uint32
LANES = 128

D_MODEL = 2048
DEPTH = 2
GRID_W = 64
HEAD_DIM = 128
EPS = 1e-6
NEG_INF = -1e30
ATTN_SCALE = HEAD_DIM ** -0.5
POOL_WINDOWS = (2, 4, 8, 16)
POOL_GROUP = 128
POOL_WIDTH = 512
WINDOW = 128
B_HEADS = 12
B_KV_HEADS = 4
C_HEADS = 16
C_KV_HEADS = 4
KV_WIDTH = 4 * HEAD_DIM
V_HEAD = 2 * HEAD_DIM
V_WIDTH = 4 * V_HEAD
LOG2E = 1.4426950408889634
Q_SCALE = ATTN_SCALE * LOG2E
ROT_AXIS = HEAD_DIM // 2
ROPE_THETA = 10000.0
N_EXPERTS = 16
CAPACITY_FACTOR = 2
COND_ROWS = 16
VMEM_LIMIT = 56 * 1024 * 1024
TOKEN_TILE = 512
ADALN_COLS = 1024
FULL_ATTN_TQ = 128
WIN_ATTN_TQ = 256
EXPERT_ROWS = 1024
EXPERT_FF = 256


def _cparams(*sem):
    return pltpu.CompilerParams(dimension_semantics=sem, vmem_limit_bytes=VMEM_LIMIT)


def _adaln_kernel(cond_ref, w_ref, b_ref, o_ref):
    cnd = cond_ref[...]
    s = cnd * (1.0 / (1.0 + jnp.exp(-cnd)))
    o_ref[0] = jnp.dot(s.astype(BF16), w_ref[0].astype(BF16), preferred_element_type=F32) + b_ref[0]


def _adaln(cond, mod_w, mod_b):
    tn = ADALN_COLS
    six_d = mod_w.shape[-1]
    return pl.pallas_call(
        _adaln_kernel,
        grid=(DEPTH, six_d // tn),
        in_specs=[
            pl.BlockSpec((COND_ROWS, D_MODEL), lambda i, j: (0, 0)),
            pl.BlockSpec((1, D_MODEL, tn), lambda i, j: (i, 0, j)),
            pl.BlockSpec((1, 1, tn), lambda i, j: (i, 0, j)),
        ],
        out_specs=pl.BlockSpec((1, COND_ROWS, tn), lambda i, j: (i, 0, j)),
        out_shape=jax.ShapeDtypeStruct((DEPTH, COND_ROWS, six_d), F32),
        compiler_params=_cparams("parallel", "parallel"),
        name="adaln",
    )(cond, mod_w, mod_b.reshape(DEPTH, 1, six_d))


def _norm_mod(x, g, shift, scale):
    ms = jnp.mean(x * x, axis=-1, keepdims=True)
    h = x * lax.rsqrt(ms + EPS) * g
    return h * (1.0 + scale) + shift


def _rope(a, cos, sin_signed):
    lane = lax.broadcasted_iota(jnp.int32, a.shape, 1)
    first = (lane & (ROT_AXIS - 1)) < (ROT_AXIS // 2)
    partner = jnp.where(first, pltpu.roll(a, HEAD_DIM - ROT_AXIS // 2, axis=1),
                        pltpu.roll(a, ROT_AXIS // 2, axis=1))
    return a * cos + partner * sin_signed


def _pack_rows(h):
    half = h.shape[1] // 2
    lo = pltpu.bitcast(h[:, :half].astype(BF16).astype(F32), U32)
    hi = pltpu.bitcast(h[:, half:].astype(BF16).astype(F32), U32)
    return (lo >> 16) | (hi & jnp.uint32(0xFFFF0000))


def _unpack_rows(w):
    lo = pltpu.bitcast(w << 16, F32).astype(BF16)
    hi = pltpu.bitcast(w & jnp.uint32(0xFFFF0000), F32).astype(BF16)
    return jnp.concatenate([lo, hi], axis=1)


def _inproj_kernel(*refs, n_pool, n_q, qk_norm, rope):
    x_ref, g_ref, sh_ref, sc_ref, w_ref = refs[:5]
    pos = 5
    if rope:
        cos_ref, sin_ref = refs[pos:pos + 2]
        pos += 2
    if qk_norm:
        qg_ref, kg_ref = refs[pos:pos + 2]
        pos += 2
    outs = refs[pos:]
    if n_pool:
        u_ref, q_ref, k_ref, v_ref = outs
    else:
        q_ref, k_ref, v_ref = outs

    hb = _norm_mod(x_ref[0], g_ref[...], sh_ref[0], sc_ref[0]).astype(BF16)
    n_heads = w_ref.shape[1] // HEAD_DIM
    group = 4
    for c0 in range(0, n_heads, group):
        acc = jnp.dot(hb, w_ref[:, c0 * HEAD_DIM:(c0 + group) * HEAD_DIM], preferred_element_type=F32)
        for j in range(group):
            c = c0 + j
            a = acc[:, j * HEAD_DIM:(j + 1) * HEAD_DIM]
            if c < n_pool:
                u_ref[0, :, c * HEAD_DIM:(c + 1) * HEAD_DIM] = a
                continue
            c -= n_pool
            is_q = c < n_q
            is_k = (not is_q) and c < n_q + 4
            if is_q or is_k:
                if qk_norm:
                    gain = qg_ref[...] if is_q else kg_ref[...]
                    sq = a * a
                    sq_hi = sq.astype(BF16)
                    sq_lo = (sq - sq_hi.astype(F32)).astype(BF16)
                    ones = jnp.ones((HEAD_DIM, HEAD_DIM), BF16)
                    ms = (jnp.dot(sq_hi, ones, preferred_element_type=F32)
                          + jnp.dot(sq_lo, ones, preferred_element_type=F32)) * (1.0 / HEAD_DIM)
                    a = a * lax.rsqrt(ms + EPS) * gain
                if rope:
                    a = _rope(a, cos_ref[...], sin_ref[...])
            if is_q:
                q_ref[0, :, c * HEAD_DIM:(c + 1) * HEAD_DIM] = (a * Q_SCALE).astype(BF16)
            elif is_k:
                c -= n_q
                k_ref[0, :, c * HEAD_DIM:(c + 1) * HEAD_DIM] = a.astype(BF16)
            else:
                c -= n_q + 4
                v_ref[0, :, c * V_HEAD:c * V_HEAD + HEAD_DIM] = a.astype(BF16)
                v_ref[0, :, c * V_HEAD + HEAD_DIM:(c + 1) * V_HEAD] = jnp.ones(a.shape, BF16)


def _inproj(x, g, shift, scale, w, *, n_pool, n_q, tm, tables=None, qk_gains=None):
    B, T, D = x.shape
    per_batch = shift.shape[0] == B
    mod_map = (lambda b, t: (b, 0, 0)) if per_batch else (lambda b, t: (0, 0, 0))
    in_specs = [
        pl.BlockSpec((1, tm, D), lambda b, t: (b, t, 0)),
        pl.BlockSpec((1, D), lambda b, t: (0, 0)),
        pl.BlockSpec((1, 1, D), mod_map),
        pl.BlockSpec((1, 1, D), mod_map),
        pl.BlockSpec(w.shape, lambda b, t: (0, 0)),
    ]
    args = [x, g.reshape(1, D), shift, scale, w]
    if tables is not None:
        in_specs += [pl.BlockSpec((tm, HEAD_DIM), lambda b, t: (t, 0))] * 2
        args += list(tables)
    if qk_gains is not None:
        in_specs += [pl.BlockSpec((1, HEAD_DIM), lambda b, t: (0, 0))] * 2
        args += [qk_gains[0].reshape(1, HEAD_DIM), qk_gains[1].reshape(1, HEAD_DIM)]
    widths = ([(n_pool * HEAD_DIM, F32)] if n_pool else []) + [
        (n_q * HEAD_DIM, BF16), (KV_WIDTH, BF16), (V_WIDTH, BF16)]
    out_specs = [pl.BlockSpec((1, tm, wd), lambda b, t: (b, t, 0)) for wd, _ in widths]
    out_shape = [jax.ShapeDtypeStruct((B, T, wd), dt) for wd, dt in widths]
    return pl.pallas_call(
        functools.partial(_inproj_kernel, n_pool=n_pool, n_q=n_q,
                          qk_norm=qk_gains is not None, rope=tables is not None),
        grid=(B, T // tm),
        in_specs=in_specs, out_specs=out_specs, out_shape=out_shape,
        compiler_params=_cparams("parallel", "parallel"),
        name="inproj",
    )(*args)


def _pool_kernel(u_ref, w_ref, s_ref, o_ref, pad_ref, *, T, chunk):
    halo = 16
    pad_ref[0:halo, :] = jnp.zeros((halo, POOL_WIDTH), F32)
    pad_ref[halo + T:2 * halo + T, :] = jnp.zeros((halo, POOL_WIDTH), F32)
    pad_ref[halo:halo + T, :] = u_ref[0]
    for g, wdw in enumerate(POOL_WINDOWS):
        cols = slice(g * POOL_GROUP, (g + 1) * POOL_GROUP)
        half = wdw // 2
        for r0 in range(0, T, chunk):
            s = pad_ref[halo + r0 - half:halo + r0 - half + chunk, cols]
            for j in range(-half + 1, half):
                s = s + pad_ref[halo + r0 + j:halo + r0 + j + chunk, cols]
            t = r0 + lax.broadcasted_iota(jnp.int32, (chunk, 1), 0)
            cnt = (jnp.minimum(t + half, T) - jnp.maximum(t - half, 0)).astype(F32)
            d = (s / cnt - u_ref[0, r0:r0 + chunk, cols]).astype(BF16)
            y = jnp.dot(d, w_ref[g], preferred_element_type=F32) * s_ref[:, cols]
            o_ref[0, r0:r0 + chunk, cols] = y.astype(BF16)


def _pool(u, pool_w, pool_scale):
    B, T, _ = u.shape
    return pl.pallas_call(
        functools.partial(_pool_kernel, T=T, chunk=min(T, 512)),
        grid=(B,),
        in_specs=[
            pl.BlockSpec((1, T, POOL_WIDTH), lambda b: (b, 0, 0)),
            pl.BlockSpec(pool_w.shape, lambda b: (0, 0, 0)),
            pl.BlockSpec((1, POOL_WIDTH), lambda b: (0, 0)),
        ],
        out_specs=pl.BlockSpec((1, T, POOL_WIDTH), lambda b: (b, 0, 0)),
        out_shape=jax.ShapeDtypeStruct((B, T, POOL_WIDTH), BF16),
        scratch_shapes=[pltpu.VMEM((T + 32, POOL_WIDTH), F32)],
        compiler_params=_cparams("parallel"),
        name="pool",
    )(u, pool_w, pool_scale.reshape(1, POOL_WIDTH))


def _stack_heads(q_ref, h, G):
    return jnp.concatenate(
        [q_ref[0, :, (h * G + g) * HEAD_DIM:(h * G + g + 1) * HEAD_DIM] for g in range(G)], axis=0)


def _qk(q, k):
    return lax.dot_general(q, k, (((1,), (1,)), ((), ())), preferred_element_type=F32)


def _pv(p, v):
    return jnp.dot(p.astype(BF16), v, preferred_element_type=F32)


def _sink_column(sink_ref, h, G, tq):
    return jnp.concatenate([jnp.full((tq, 1), sink_ref[h * G + g] * LOG2E, F32) for g in range(G)], axis=0)


def _store_heads(o_ref, o, h, G, tq):
    for g in range(G):
        o_ref[0, :, (h * G + g) * HEAD_DIM:(h * G + g + 1) * HEAD_DIM] = o[g * tq:(g + 1) * tq].astype(BF16)


def _lane_tile_max(s):
    parts = [s[:, t * HEAD_DIM:(t + 1) * HEAD_DIM] for t in range(s.shape[1] // HEAD_DIM)]
    while len(parts) > 1:
        parts = [jnp.maximum(parts[i], parts[i + 1]) if i + 1 < len(parts) else parts[i]
                 for i in range(0, len(parts), 2)]
    return parts[0]


def _win_attn_kernel(sink_ref, q_ref, k_ref, v_ref, kc_ref, vc_ref, o_ref, s_ref, p_ref, m_ref, *, T, tq):
    G = B_HEADS // B_KV_HEADS
    R = G * tq
    kw_len = tq + 2 * WINDOW
    Tc = kc_ref.shape[1]
    n_keys = kw_len + Tc
    q0 = pl.program_id(1) * tq
    start = pl.multiple_of(jnp.clip(q0 - WINDOW, 0, T - kw_len), WINDOW)
    delta = q0 - start
    r = lax.broadcasted_iota(I32, (R, kw_len), 0) & (tq - 1)
    c = lax.broadcasted_iota(I32, (R, kw_len), 1)
    bias = jnp.where(jnp.abs(r + delta - c) <= WINDOW, 0.0, NEG_INF)

    def sink_tile(h):
        return jnp.concatenate([jnp.full((tq, HEAD_DIM), sink_ref[h * G + g] * LOG2E, F32) for g in range(G)],
                               axis=0)

    def scores(h):
        hs = slice(h * HEAD_DIM, (h + 1) * HEAD_DIM)
        qs = _stack_heads(q_ref, h, G)
        s_loc = _qk(qs, k_ref[0, pl.ds(start, kw_len), hs]) + bias
        s_ctx = _qk(qs, kc_ref[0, :, hs])
        s_ref[h, :, 0:kw_len] = s_loc
        s_ref[h, :, kw_len:n_keys] = s_ctx
        m_t = jnp.maximum(jnp.maximum(_lane_tile_max(s_loc), _lane_tile_max(s_ctx)), sink_tile(h))
        m_ref[h] = jnp.broadcast_to(jnp.max(m_t, axis=-1, keepdims=True), (R, HEAD_DIM))

    def weights(h):
        mb = m_ref[h]
        for t in range(n_keys // HEAD_DIM):
            cols = slice(t * HEAD_DIM, (t + 1) * HEAD_DIM)
            p_ref[h, :, cols] = jnp.exp2(s_ref[h, :, cols] - mb).astype(BF16)

    def values(h):
        vs = slice(h * V_HEAD, (h + 1) * V_HEAD)
        oa = (jnp.dot(p_ref[h, :, 0:kw_len], v_ref[0, pl.ds(start, kw_len), vs], preferred_element_type=F32)
              + jnp.dot(p_ref[h, :, kw_len:n_keys], vc_ref[0, :, vs], preferred_element_type=F32))
        o = oa[:, :HEAD_DIM] / (oa[:, HEAD_DIM:] + jnp.exp2(sink_tile(h) - m_ref[h]))
        _store_heads(o_ref, o, h, G, tq)

    scores(0)
    for h in range(B_KV_HEADS):
        if h + 1 < B_KV_HEADS:
            scores(h + 1)
        weights(h)
        values(h)


def _win_attn(sink, q, k, v, kc, vc, *, tq=WIN_ATTN_TQ):
    B, T, qw = q.shape
    Tc = kc.shape[1]
    return pl.pallas_call(
        functools.partial(_win_attn_kernel, T=T, tq=tq),
        grid=(B, T // tq),
        in_specs=[
            pl.BlockSpec(memory_space=pltpu.SMEM),
            pl.BlockSpec((1, tq, qw), lambda b, t: (b, t, 0)),
            pl.BlockSpec((1, T, KV_WIDTH), lambda b, t: (b, 0, 0)),
            pl.BlockSpec((1, T, V_WIDTH), lambda b, t: (b, 0, 0)),
            pl.BlockSpec((1, Tc, KV_WIDTH), lambda b, t: (b, 0, 0)),
            pl.BlockSpec((1, Tc, V_WIDTH), lambda b, t: (b, 0, 0)),
        ],
        out_specs=pl.BlockSpec((1, tq, qw), lambda b, t: (b, t, 0)),
        out_shape=jax.ShapeDtypeStruct((B, T, qw), BF16),
        scratch_shapes=[pltpu.VMEM((B_KV_HEADS, (B_HEADS // B_KV_HEADS) * tq, tq + 2 * WINDOW + Tc), F32),
                        pltpu.VMEM((B_KV_HEADS, (B_HEADS // B_KV_HEADS) * tq, tq + 2 * WINDOW + Tc), BF16),
                        pltpu.VMEM((B_KV_HEADS, (B_HEADS // B_KV_HEADS) * tq, HEAD_DIM), F32)],
        compiler_params=_cparams("parallel", "arbitrary"),
        name="win_attn",
    )(sink, q, k, v, kc, vc)


def _ctx_attn_kernel(*refs, G, has_sink):
    if has_sink:
        sink_ref, q_ref, k_ref, v_ref, o_ref = refs
    else:
        q_ref, k_ref, v_ref, o_ref = refs
    tq = q_ref.shape[1]
    for h in range(q_ref.shape[2] // (G * HEAD_DIM)):
        qs = _stack_heads(q_ref, h, G)
        s = _qk(qs, k_ref[0, :, h * HEAD_DIM:(h + 1) * HEAD_DIM])
        m = jnp.max(s, axis=-1, keepdims=True)
        if has_sink:
            sink = _sink_column(sink_ref, h, G, tq)
            m = jnp.maximum(m, sink)
        oa = _pv(jnp.exp2(s - m), v_ref[0, :, h * V_HEAD:(h + 1) * V_HEAD])
        den = oa[:, HEAD_DIM:]
        if has_sink:
            den = den + jnp.exp2(sink - m)
        _store_heads(o_ref, oa[:, :HEAD_DIM] / den, h, G, tq)


def _ctx_attn(q, k, v, *, G, sink=None):
    B, Tc, qw = q.shape
    in_specs, args = [], []
    if sink is not None:
        in_specs.append(pl.BlockSpec(memory_space=pltpu.SMEM))
        args.append(sink)
    in_specs += [pl.BlockSpec((1, Tc, a.shape[2]), lambda b: (b, 0, 0)) for a in (q, k, v)]
    return pl.pallas_call(
        functools.partial(_ctx_attn_kernel, G=G, has_sink=sink is not None),
        grid=(B,),
        in_specs=in_specs,
        out_specs=pl.BlockSpec((1, Tc, qw), lambda b: (b, 0, 0)),
        out_shape=jax.ShapeDtypeStruct((B, Tc, qw), BF16),
        compiler_params=_cparams("parallel"),
        name="ctx_attn",
    )(*args, q, k, v)


def _full_attn_kernel(q_ref, kc_ref, vc_ref, k_ref, v_ref, o_ref, s_ref, p_ref, m_ref, *, G, row_block, key_chunk):
    tq = q_ref.shape[1]
    R = G * tq
    half = R // 2
    Tc, Tk = kc_ref.shape[1], s_ref.shape[2]
    n_lane_tiles = Tk // HEAD_DIM
    n_heads = q_ref.shape[2] // (G * HEAD_DIM)

    def scores(h):
        hs = slice(h * HEAD_DIM, (h + 1) * HEAD_DIM)
        qs = _stack_heads(q_ref, h, G)
        for r0 in (0, half):
            q_half = qs[r0:r0 + half]
            s = _qk(q_half, kc_ref[0, :, hs])
            s_ref[h % 2, r0:r0 + half, 0:Tc] = s
            m_run = _lane_tile_max(s)
            for c0 in range(0, Tk - Tc, key_chunk):
                s = _qk(q_half, k_ref[0, c0:c0 + key_chunk, hs])
                s_ref[h % 2, r0:r0 + half, Tc + c0:Tc + c0 + key_chunk] = s
                m_run = jnp.maximum(m_run, _lane_tile_max(s))
            m_ref[h % 2, r0:r0 + half, :] = jnp.broadcast_to(
                jnp.max(m_run, axis=-1, keepdims=True), (half, HEAD_DIM))

    def weights(h):
        for r0 in range(0, R, row_block):
            mb = m_ref[h % 2, r0:r0 + row_block, :]
            for t in range(n_lane_tiles):
                cols = slice(t * HEAD_DIM, (t + 1) * HEAD_DIM)
                p_ref[h % 2, r0:r0 + row_block, cols] = jnp.exp2(
                    s_ref[h % 2, r0:r0 + row_block, cols] - mb).astype(BF16)

    def values(h):
        vs = slice(h * V_HEAD, (h + 1) * V_HEAD)
        for r0 in (0, half):
            oa = (jnp.dot(p_ref[h % 2, r0:r0 + half, 0:Tc], vc_ref[0, :, vs], preferred_element_type=F32)
                  + jnp.dot(p_ref[h % 2, r0:r0 + half, Tc:Tk], v_ref[0, :, vs], preferred_element_type=F32))
            o = oa[:, :HEAD_DIM] / oa[:, HEAD_DIM:]
            for g in range(G):
                lo, hi = g * tq, (g + 1) * tq
                a, bnd = max(lo, r0), min(hi, r0 + half)
                if a < bnd:
                    o_ref[0, a - lo:bnd - lo, (h * G + g) * HEAD_DIM:(h * G + g + 1) * HEAD_DIM] = (
                        o[a - r0:bnd - r0].astype(BF16))

    scores(0)
    for h in range(n_heads):
        if h + 1 < n_heads:
            scores(h + 1)
        weights(h)
        values(h)


def _full_attn(q, kc, vc, k, v, *, G, tq):
    B, T, qw = q.shape
    Tc = kc.shape[1]
    Tk = Tc + T
    return pl.pallas_call(
        functools.partial(_full_attn_kernel, G=G, row_block=32, key_chunk=512),
        grid=(B, T // tq),
        in_specs=[
            pl.BlockSpec((1, tq, qw), lambda b, t: (b, t, 0)),
            pl.BlockSpec((1, Tc, KV_WIDTH), lambda b, t: (b, 0, 0), pipeline_mode=pl.Buffered(1)),
            pl.BlockSpec((1, Tc, V_WIDTH), lambda b, t: (b, 0, 0), pipeline_mode=pl.Buffered(1)),
            pl.BlockSpec((1, T, KV_WIDTH), lambda b, t: (b, 0, 0), pipeline_mode=pl.Buffered(1)),
            pl.BlockSpec((1, T, V_WIDTH), lambda b, t: (b, 0, 0), pipeline_mode=pl.Buffered(1)),
        ],
        out_specs=pl.BlockSpec((1, tq, qw), lambda b, t: (b, t, 0)),
        out_shape=jax.ShapeDtypeStruct((B, T, qw), BF16),
        scratch_shapes=[pltpu.VMEM((2, G * tq, Tk), F32), pltpu.VMEM((2, G * tq, Tk), BF16),
                        pltpu.VMEM((2, G * tq, HEAD_DIM), F32)],
        compiler_params=_cparams("parallel", "arbitrary"),
        name="full_attn",
    )(q, kc, vc, k, v)


def _outproj_kernel(*refs, n_y):
    y_refs = refs[:n_y]
    (w_ref, x_ref, g1_ref, ng_ref, sh_ref, sc_ref, rw_ref, xo_ref, h_ref, lg_ref) = refs[n_y:]
    acc = None
    row = 0
    for y_ref in y_refs:
        wd = y_ref.shape[2]
        part = jnp.dot(y_ref[0], w_ref[row:row + wd, :], preferred_element_type=F32)
        acc = part if acc is None else acc + part
        row += wd
    x = x_ref[0] + g1_ref[0] * acc
    xo_ref[0] = x
    h = _norm_mod(x, ng_ref[...], sh_ref[0], sc_ref[0])
    h_ref[0] = _pack_rows(h)
    E = N_EXPERTS
    h_hi = h.astype(BF16)
    h_lo = (h - h_hi.astype(F32)).astype(BF16)
    dot = functools.partial(jnp.dot, preferred_element_type=F32)
    p_hi = dot(h_hi, rw_ref[...])
    lg_ref[0] = p_hi[:, :E] + (p_hi[:, E:] + dot(h_lo, rw_ref[:, :E]))


def _outproj(ys, w, x, g1, ng, shift, scale, rw3, *, tm):
    B, T, D = x.shape
    per_batch = g1.shape[0] == B
    mod_map = (lambda b, t: (b, 0, 0)) if per_batch else (lambda b, t: (0, 0, 0))
    tile = lambda wd: pl.BlockSpec((1, tm, wd), lambda b, t: (b, t, 0))
    in_specs = [tile(y.shape[2]) for y in ys] + [
        pl.BlockSpec(w.shape, lambda b, t: (0, 0)),
        tile(D),
        pl.BlockSpec((1, 1, D), mod_map),
        pl.BlockSpec((1, D), lambda b, t: (0, 0)),
        pl.BlockSpec((1, 1, D), mod_map),
        pl.BlockSpec((1, 1, D), mod_map),
        pl.BlockSpec(rw3.shape, lambda b, t: (0, 0)),
    ]
    return pl.pallas_call(
        functools.partial(_outproj_kernel, n_y=len(ys)),
        grid=(B, T // tm),
        in_specs=in_specs,
        out_specs=[tile(D), tile(D // 2), tile(N_EXPERTS)],
        out_shape=[jax.ShapeDtypeStruct((B, T, D), F32),
                   jax.ShapeDtypeStruct((B, T, D // 2), U32),
                   jax.ShapeDtypeStruct((B, T, N_EXPERTS), F32)],
        compiler_params=_cparams("parallel", "parallel"),
        name="outproj",
    )(*ys, w, x, g1, ng.reshape(1, D), shift, scale, rw3)


def _token_prefix(x, tri):
    T = x.shape[0]
    xb = x.astype(BF16)
    local = [jnp.dot(tri, xb[j:j + LANES], preferred_element_type=F32) for j in range(0, T, LANES)]
    out, run = [], None
    for blk in local:
        blk = blk if run is None else blk + run
        run = blk[LANES - 1:LANES, :]
        out.append(blk)
    return jnp.concatenate(out, axis=0)


def _split_exact(a, n):
    parts, rest = [], a
    for _ in range(n):
        p = rest.astype(BF16)
        parts.append(p)
        rest = rest - p.astype(F32)
    return parts


def _to_lanes(x, eye):
    nt = lambda p: lax.dot_general(eye, p, (((1,), (1,)), ((), ())), preferred_element_type=F32)
    parts = _split_exact(x, 3)
    return (nt(parts[0]) + nt(parts[1])) + nt(parts[2])


def _route_kernel(lg_ref, idx_ref, gate_ref, sel_ref, slot_ref, lo_ref, *scratch, T, cap, tile, hier):
    E = N_EXPERTS
    lg = lg_ref[0]
    ex = jnp.exp(lg - jnp.max(lg, axis=-1, keepdims=True))
    aff = ex / jnp.sum(ex, axis=-1, keepdims=True)
    bits = pltpu.bitcast(aff, I32)

    groups = LANES // E if T >= 8 * LANES else 1
    aff_p = jnp.concatenate([aff[k * (T // groups):(k + 1) * (T // groups), :] for k in range(groups)], axis=1)

    def bisect(_, c):
        lo, hi = c
        mid = lo + ((hi - lo) >> 1)
        cnt = jnp.sum(jnp.where(aff_p >= pltpu.bitcast(mid, F32), 1.0, 0.0), axis=0, keepdims=True)
        shift = groups * E // 2
        while shift >= E:
            cnt = cnt + pltpu.roll(cnt, shift, 1)
            shift //= 2
        ge = cnt >= cap
        return jnp.where(ge, mid, lo), jnp.where(ge, hi, mid)

    thr, _ = lax.fori_loop(0, 31, bisect, (jnp.zeros((1, groups * E), I32),
                                           jnp.full((1, groups * E), 0x3F800001, I32)))
    thr = thr[:, :E]
    thr_f = pltpu.bitcast(thr, F32)
    gt = aff > thr_f
    eq = aff == thr_f
    r = lax.broadcasted_iota(I32, (LANES, LANES), 0)
    cidx = lax.broadcasted_iota(I32, (LANES, LANES), 1)
    tri = jnp.where(cidx <= r, 1.0, 0.0).astype(BF16)
    eq_f = jnp.where(eq, 1.0, 0.0)
    need = cap - jnp.sum(jnp.where(gt, 1.0, 0.0), axis=0, keepdims=True)
    tie_rank = _token_prefix(eq_f, tri) - eq_f
    sel = gt | (eq & (tie_rank < need))
    sel_f = jnp.where(sel, 1.0, 0.0)
    pos = _token_prefix(sel_f, tri)
    sel_ref[0] = sel_f
    slot_ref[0] = pos - 1.0
    lo_ref[0, 0:1, :] = jnp.zeros((1, E), I32)
    for j in range(1, T // tile + 1):
        lo_ref[0, j:j + 1, :] = pos[j * tile - 1:j * tile, :].astype(I32)

    if hier:
        cnt_ref, aff_ref = scratch
        nblk = T // LANES
        eye = jnp.where(lax.broadcasted_iota(I32, (E, E), 0) == lax.broadcasted_iota(I32, (E, E), 1),
                        1.0, 0.0).astype(BF16)
        for j in range(nblk):
            cnt_ref[j] = _to_lanes(pos[j * LANES:(j + 1) * LANES, :], eye)
            aff_ref[j] = _to_lanes(aff[j * LANES:(j + 1) * LANES, :], eye)
        ends = jnp.concatenate([pos[(j + 1) * LANES - 1:(j + 1) * LANES, :] for j in range(nblk)], axis=0)
        starts = jnp.concatenate([jnp.zeros((1, E), F32), ends[:nblk - 1]], axis=0)
        ends_t, starts_t = _to_lanes(ends, eye), _to_lanes(starts, eye)
        s_col = lax.broadcasted_iota(I32, (cap, nblk), 0).astype(F32)
        blk_id = lax.broadcasted_iota(I32, (cap, nblk), 1).astype(F32)
        s_lane = lax.broadcasted_iota(I32, (cap, LANES), 0).astype(F32)
        lane = lax.broadcasted_iota(I32, (cap, LANES), 1).astype(F32)
        for e in range(E):
            in_blk = (starts_t[e:e + 1, :] <= s_col) & (s_col < ends_t[e:e + 1, :])
            hot = jnp.where(in_blk, 1.0, 0.0).astype(BF16)
            base = jnp.sum(jnp.where(in_blk, blk_id, 0.0), axis=-1, keepdims=True) * float(LANES)
            c_hi, c_lo = _split_exact(cnt_ref[:, e, :], 2)
            counts = (jnp.dot(hot, c_hi, preferred_element_type=F32)
                      + jnp.dot(hot, c_lo, preferred_element_type=F32))
            local = jnp.sum(jnp.where(counts <= s_lane, 1.0, 0.0), axis=-1, keepdims=True)
            a_parts = _split_exact(aff_ref[:, e, :], 3)
            affs = ((jnp.dot(hot, a_parts[0], preferred_element_type=F32)
                     + jnp.dot(hot, a_parts[1], preferred_element_type=F32))
                    + jnp.dot(hot, a_parts[2], preferred_element_type=F32))
            idx_ref[0, e] = (base + local).astype(I32)
            gate_ref[0, e] = jnp.sum(jnp.where(lane == local, affs, 0.0), axis=-1, keepdims=True)
    else:
        aff_ref, pos_ref = scratch
        aff_ref[...] = aff
        pos_ref[...] = jnp.where(sel, pos, 0.0)
        for e in range(E):
            p_col = jnp.broadcast_to(pos_ref[:, e:e + 1], (T, cap))
            a_col = jnp.broadcast_to(aff_ref[:, e:e + 1], (T, cap))
            tok = lax.broadcasted_iota(I32, (T, cap), 0).astype(F32)
            hit = p_col == (lax.broadcasted_iota(I32, (T, cap), 1) + 1).astype(F32)
            eye_c = lax.broadcasted_iota(I32, (cap, cap), 0) == lax.broadcasted_iota(I32, (cap, cap), 1)
            t_row = jnp.sum(jnp.where(hit, tok, 0.0), axis=0, keepdims=True)
            g_row = jnp.sum(jnp.where(hit, a_col, 0.0), axis=0, keepdims=True)
            idx_ref[0, e] = jnp.sum(jnp.where(eye_c, jnp.broadcast_to(t_row, (cap, cap)), 0.0), axis=-1,
                                    keepdims=True).astype(I32)
            gate_ref[0, e] = jnp.sum(jnp.where(eye_c, jnp.broadcast_to(g_row, (cap, cap)), 0.0), axis=-1,
                                     keepdims=True)


def _route(logits, *, cap, tile):
    B, T, E = logits.shape
    nb = T // tile + 1
    hier = T % LANES == 0 and T // LANES >= 16
    tok_spec = pl.BlockSpec((1, T, E), lambda b: (b, 0, 0))
    slot_spec = pl.BlockSpec((1, E, cap, 1), lambda b: (b, 0, 0, 0))
    scratch = ([pltpu.VMEM((T // LANES, E, LANES), F32)] * 2 if hier else [pltpu.VMEM((T, E), F32)] * 2)
    return pl.pallas_call(
        functools.partial(_route_kernel, T=T, cap=cap, tile=tile, hier=hier),
        grid=(B,),
        in_specs=[tok_spec],
        out_specs=[slot_spec, slot_spec, tok_spec, tok_spec, pl.BlockSpec((1, nb, E), lambda b: (b, 0, 0))],
        out_shape=[jax.ShapeDtypeStruct((B, E, cap, 1), I32), jax.ShapeDtypeStruct((B, E, cap, 1), F32),
                   jax.ShapeDtypeStruct((B, T, E), F32), jax.ShapeDtypeStruct((B, T, E), F32),
                   jax.ShapeDtypeStruct((B, nb, E), I32)],
        scratch_shapes=scratch,
        compiler_params=_cparams("arbitrary"),
        name="route",
    )(logits)


def _expert_kernel(idx_ref, h_hbm, gate_ref, wg_ref, wu_ref, wd_ref, o_ref, rows_ref, xs_ref, acc_ref, sem,
                   *, tm, n_tiles):
    n, f = pl.program_id(0), pl.program_id(1)
    nf = pl.num_programs(1)
    per = tm // nf
    slot = lax.rem(n, 2)

    def row_copy(tile, r, buf):
        return pltpu.make_async_copy(h_hbm.at[pl.ds(idx_ref[tile * tm + r], 1), :],
                                     rows_ref.at[buf, pl.ds(r, 1), :], sem.at[buf])

    def tile_wait(buf):
        pltpu.make_async_copy(h_hbm.at[pl.ds(0, tm), :], rows_ref.at[buf], sem.at[buf]).wait()

    @pl.when((n == 0) & (f == 0))
    def _():
        for r in range(tm):
            row_copy(0, r, 0).start(priority=r % 2)

    @pl.when(f == 0)
    def _():
        tile_wait(slot)
        xs_ref[...] = _unpack_rows(rows_ref[slot])
        acc_ref[...] = jnp.zeros(acc_ref.shape, F32)

    nxt = jnp.minimum(n + 1, n_tiles - 1)
    for i in range(per):
        row_copy(nxt, f * per + i, 1 - slot).start(priority=i % 2)

    xs = xs_ref[...]
    a = jnp.dot(xs, wg_ref[0, 0].astype(BF16), preferred_element_type=F32)
    u = jnp.dot(xs, wu_ref[0, 0].astype(BF16), preferred_element_type=F32)
    act = (a * (1.0 / (1.0 + jnp.exp(-a))) * u).astype(BF16)
    acc_ref[...] += jnp.dot(act, wd_ref[0, 0].astype(BF16), preferred_element_type=F32)

    @pl.when(f == nf - 1)
    def _():
        blk = min(tm, LANES)
        eye = lax.broadcasted_iota(I32, (blk, blk), 0) == lax.broadcasted_iota(I32, (blk, blk), 1)
        for r0 in range(0, tm, blk):
            g_row = jnp.broadcast_to(gate_ref[0, :, r0:r0 + blk], (blk, blk))
            g_col = jnp.sum(jnp.where(eye, g_row, 0.0), axis=-1, keepdims=True)
            o_ref[r0:r0 + blk, :] = (acc_ref[r0:r0 + blk, :] * g_col).astype(BF16)

    @pl.when((n == n_tiles - 1) & (f == nf - 1))
    def _():
        tile_wait(1 - slot)


def _experts(idx, h, gate, wg, wu, wd, layer, *, tm, tf=EXPERT_FF):
    N, half = h.shape
    D = 2 * half
    _, E, _, F = wg.shape
    n_tiles = idx.shape[0] // tm
    per_e = n_tiles // E
    nf = F // tf
    assert tm % nf == 0
    return pl.pallas_call(
        functools.partial(_expert_kernel, tm=tm, n_tiles=n_tiles),
        grid_spec=pltpu.PrefetchScalarGridSpec(
            num_scalar_prefetch=1,
            grid=(n_tiles, nf),
            in_specs=[
                pl.BlockSpec(memory_space=pl.ANY),
                pl.BlockSpec((1, 1, tm), lambda n, f, idx: (n, 0, 0)),
                pl.BlockSpec((1, 1, D, tf), lambda n, f, idx: (layer, n // per_e, 0, f)),
                pl.BlockSpec((1, 1, D, tf), lambda n, f, idx: (layer, n // per_e, 0, f)),
                pl.BlockSpec((1, 1, tf, D), lambda n, f, idx: (layer, n // per_e, f, 0)),
            ],
            out_specs=pl.BlockSpec((tm, D), lambda n, f, idx: (n, 0)),
            scratch_shapes=[pltpu.VMEM((2, tm, half), U32), pltpu.VMEM((tm, D), BF16), pltpu.VMEM((tm, D), F32),
                            pltpu.SemaphoreType.DMA((2,))]),
        out_shape=jax.ShapeDtypeStruct((n_tiles * tm, D), BF16),
        compiler_params=_cparams("arbitrary", "arbitrary"),
        name="experts",
    )(idx, h, gate, wg, wu, wd)


def _combine_kernel(lo_ref, x_ref, g_ref, sel_ref, slot_ref, ye_hbm, *rest, cap, win, n_tiles, seg_stride, final):
    if final:
        fg_ref, o_ref, stage_ref, extra_ref, hot_ref, sem = rest
    else:
        o_ref, stage_ref, extra_ref, hot_ref, sem = rest
    E = N_EXPERTS
    b, t = pl.program_id(0), pl.program_id(1)
    step = b * n_tiles + t
    n_steps = pl.num_programs(0) * n_tiles
    buf = lax.rem(step, 2)
    tm = x_ref.shape[1]

    def seg_lo(bb, tt, e):
        return lo_ref[(bb * (n_tiles + 1) + tt) * E + e]

    def win_start(bb, tt, e):
        w0 = jnp.minimum((seg_lo(bb, tt, e) // 16) * 16, cap - win)
        return pl.multiple_of((e * seg_stride + bb) * cap + w0, 16), w0

    def window_copy(bb, tt, e, k):
        row, _ = win_start(bb, tt, e)
        return pltpu.make_async_copy(ye_hbm.at[pl.ds(row, win), :], stage_ref.at[k, pl.ds(e * win, win), :],
                                     sem.at[k])

    @pl.when(step == 0)
    def _():
        for e in range(E):
            window_copy(b, t, e, 0).start()

    @pl.when(step + 1 < n_steps)
    def _():
        nb = jnp.where(t + 1 < n_tiles, b, b + 1)
        nt = jnp.where(t + 1 < n_tiles, t + 1, 0)
        for e in range(E):
            window_copy(nb, nt, e, 1 - buf).start()

    for e in range(E):
        window_copy(b, t, e, buf).wait()

    sel = sel_ref[0] > 0.5
    slot = slot_ref[0]
    lane = lax.broadcasted_iota(I32, (tm, win), 1).astype(F32)
    for e in range(E):
        _, w0 = win_start(b, t, e)
        rel = jnp.where(sel[:, e:e + 1], slot[:, e:e + 1] - w0.astype(F32), -1.0)
        hot_ref[:, e * win:(e + 1) * win] = jnp.where(
            jnp.broadcast_to(rel, (tm, win)) == lane, 1.0, 0.0).astype(BF16)
    D = x_ref.shape[2]
    cw = min(D, 512)
    for c0 in range(0, D, cw):
        cols = slice(c0, c0 + cw)
        moe = jnp.dot(hot_ref[...], stage_ref[buf, :, cols], preferred_element_type=F32)
        o_ref[0, :, cols] = x_ref[0, :, cols] + g_ref[0, :, cols] * moe

    if win < cap:
        for e in range(E):
            _, w0 = win_start(b, t, e)
            hi = seg_lo(b, t + 1, e)
            n_extra = jnp.maximum(hi - (w0 + win) + win - 1, 0) // win

            def extra(k, carry, e=e, w0=w0):
                c0 = w0 + k * win
                ck = jnp.minimum(c0, cap - win)
                row = pl.multiple_of((e * seg_stride + b) * cap + ck, 16)
                cp = pltpu.make_async_copy(ye_hbm.at[pl.ds(row, win), :], extra_ref, sem.at[2])
                cp.start()
                cp.wait()
                s_e = slot[:, e:e + 1]
                ok = sel[:, e:e + 1] & (s_e >= c0.astype(F32)) & (s_e < (c0 + win).astype(F32))
                rel = jnp.where(ok, s_e - ck.astype(F32), -1.0)
                h1 = jnp.where(jnp.broadcast_to(rel, (tm, win)) == lane, 1.0, 0.0).astype(BF16)
                o_ref[0] += g_ref[0] * jnp.dot(h1, extra_ref[...], preferred_element_type=F32)
                return carry

            lax.fori_loop(1, n_extra + 1, extra, 0)

    if final:
        rb = min(tm, 64)

        def norm_rows(i, carry):
            rows = pl.ds(pl.multiple_of(i * rb, rb), rb)
            y = o_ref[0, rows, :]
            o_ref[0, rows, :] = y * lax.rsqrt(jnp.mean(y * y, axis=-1, keepdims=True) + EPS) * fg_ref[...]
            return carry

        lax.fori_loop(0, tm // rb, norm_rows, 0)


def _combine(lo, x, g2, sel, slot, ye, *, cap, tile, seg_stride, final_g=None):
    B, T, D = x.shape
    n_tiles = T // tile
    win = min(cap, LANES)
    per_batch = g2.shape[0] == B
    mod_map = (lambda b, t, lo: (b, 0, 0)) if per_batch else (lambda b, t, lo: (0, 0, 0))
    tok = lambda w: pl.BlockSpec((1, tile, w), lambda b, t, lo: (b, t, 0))
    in_specs = [tok(D), pl.BlockSpec((1, 1, D), mod_map), tok(N_EXPERTS), tok(N_EXPERTS),
                pl.BlockSpec(memory_space=pl.ANY)]
    args = [x, g2, sel, slot, ye]
    if final_g is not None:
        in_specs.append(pl.BlockSpec((1, D), lambda b, t, lo: (0, 0)))
        args.append(final_g.reshape(1, D))
    return pl.pallas_call(
        functools.partial(_combine_kernel, cap=cap, win=win, n_tiles=n_tiles, seg_stride=seg_stride,
                          final=final_g is not None),
        grid_spec=pltpu.PrefetchScalarGridSpec(
            num_scalar_prefetch=1,
            grid=(B, n_tiles),
            in_specs=in_specs,
            out_specs=tok(D),
            scratch_shapes=[pltpu.VMEM((2, N_EXPERTS * win, D), BF16), pltpu.VMEM((win, D), BF16),
                            pltpu.VMEM((tile, N_EXPERTS * win), BF16), pltpu.SemaphoreType.DMA((3,))]),
        out_shape=jax.ShapeDtypeStruct((B, T, D), F32),
        compiler_params=_cparams("arbitrary", "arbitrary"),
        name="combine",
    )(lo.reshape(-1), *args)


def _rope_tables(n_tokens):
    rows = n_tokens // GRID_W
    row = jnp.broadcast_to(jnp.arange(rows)[:, None], (rows, GRID_W)).reshape(-1).astype(F32)
    col = jnp.broadcast_to(jnp.arange(GRID_W)[None, :], (rows, GRID_W)).reshape(-1).astype(F32)
    inv = ROPE_THETA ** (-jnp.arange(0, ROT_AXIS, 2, dtype=F32) / ROT_AXIS)
    ang_r, ang_c = row[:, None] * inv, col[:, None] * inv
    cos = jnp.concatenate([jnp.cos(ang_r)] * 2 + [jnp.cos(ang_c)] * 2, axis=1)
    sin = jnp.concatenate([-jnp.sin(ang_r), jnp.sin(ang_r), -jnp.sin(ang_c), jnp.sin(ang_c)], axis=1)
    return cos, sin


def _moe(h, logits, x, g2, wg, wu, wd, layer, *, tile, final_g=None):
    B, T, D = x.shape
    cap = CAPACITY_FACTOR * T // N_EXPERTS
    idx, gate, sel, slot, lo = _route(logits, cap=cap, tile=tile)
    rows = jnp.swapaxes(idx[..., 0] + (jnp.arange(B, dtype=I32) * T)[:, None, None], 0, 1).reshape(-1)
    tm = min(B * cap, EXPERT_ROWS)
    gate = jnp.swapaxes(gate[..., 0], 0, 1).reshape(-1, 1, tm)
    ye = _experts(rows, h.reshape(B * T, D // 2), gate, wg, wu, wd, layer, tm=tm)
    return _combine(lo, x, g2, sel, slot, ye, cap=cap, tile=tile, seg_stride=B, final_g=final_g)


def kernel(x, c, ctx, c_ctx, mod_w, mod_b, norm_g, w_out, router_w, expert_w_gate, expert_w_up,
           expert_w_down, ab_w_in, pool_w, pool_scale, ab_sink, c_w_in, c_q_norm_g, c_k_norm_g,
           final_norm_g):
    B, T, D = x.shape
    Tc = ctx.shape[1]
    tables = _rope_tables(T)

    cond = jnp.zeros((COND_ROWS, D), F32).at[:B].set(c).at[B].set(c_ctx)
    mod = _adaln(cond, mod_w, mod_b)

    for i in range(DEPTH):
        last = i == DEPTH - 1
        j = i // 2
        mx = [mod[i, :B, k * D:(k + 1) * D].reshape(B, 1, D) for k in range(6)]
        mc = [mod[i, B:B + 1, k * D:(k + 1) * D].reshape(1, 1, D) for k in range(6)]
        rw_hi = router_w[i].astype(BF16)
        rw3 = jnp.concatenate([rw_hi, (router_w[i] - rw_hi.astype(F32)).astype(BF16)], axis=1)
        wo = w_out[i].astype(BF16)
        wg, wu, wd = expert_w_gate, expert_w_up, expert_w_down
        if i % 2 == 0:
            w_in = ab_w_in[j].astype(BF16)
            u, q, k, v = _inproj(x, norm_g[i, 0], mx[0], mx[1], w_in, n_pool=4, n_q=B_HEADS,
                                 tm=TOKEN_TILE, tables=tables)
            uc, qc, kc, vc = _inproj(ctx, norm_g[i, 0], mc[0], mc[1], w_in, n_pool=4, n_q=B_HEADS, tm=Tc)
            pw = pool_w[j].astype(BF16)
            ys = [_pool(u, pw, pool_scale[j]), _win_attn(ab_sink[j], q, k, v, kc, vc)]
            if not last:
                ycs = [_pool(uc, pw, pool_scale[j]),
                       _ctx_attn(qc, kc, vc, G=B_HEADS // B_KV_HEADS, sink=ab_sink[j])]
        else:
            w_in = c_w_in[j].astype(BF16)
            gains = (c_q_norm_g[j], c_k_norm_g[j])
            q, k, v = _inproj(x, norm_g[i, 0], mx[0], mx[1], w_in, n_pool=0, n_q=C_HEADS,
                              tm=TOKEN_TILE, tables=tables, qk_gains=gains)
            qc, kc, vc = _inproj(ctx, norm_g[i, 0], mc[0], mc[1], w_in, n_pool=0, n_q=C_HEADS,
                                 tm=Tc, qk_gains=gains)
            ys = [_full_attn(q, kc, vc, k, v, G=C_HEADS // C_KV_HEADS, tq=FULL_ATTN_TQ)]
            if not last:
                ycs = [_ctx_attn(qc, kc, vc, G=C_HEADS // C_KV_HEADS)]

        x, h2, lg = _outproj(ys, wo, x, mx[2], norm_g[i, 1], mx[3], mx[4], rw3, tm=TOKEN_TILE)
        x = _moe(h2, lg, x, mx[5], wg, wu, wd, i, tile=TOKEN_TILE, final_g=final_norm_g if last else None)
        if not last:
            ctx, h2c, lgc = _outproj(ycs, wo, ctx, mc[2], norm_g[i, 1], mc[3], mc[4], rw3, tm=Tc)
            ctx = _moe(h2c, lgc, ctx, mc[5], wg, wu, wd, i, tile=Tc)

    return x
```
